```python
import jax, jax.numpy as jnp
from jax import lax
import numpy as np

D_MODEL = 2048
BATCH = 16
SEQ = 256
DEPTH = 1
DEC_BATCH = 2
DEC_SEQ = 1024
PAST_LEN = 512

GRID_W = 64
CONV_WIDTH = 1024
CONV_K = 3
RET_HEADS = 8
RET_DK = 128
RET_DV = 256
RET_QK = RET_HEADS * RET_DK
RET_V = RET_HEADS * RET_DV
CHUNK = 128
N_EXPERTS = 32
TOP_K = 4
D_FF = 2048
SWIGLU_LIMIT = 7.0
SWIGLU_ALPHA = 1.702
ROPE_THETA = 10000.0
EPS = 1e-6
N_MOD = 6
IN_SIZES = [CONV_WIDTH, CONV_WIDTH, CONV_WIDTH, RET_QK, RET_QK, RET_V, RET_V, D_MODEL, D_MODEL]
IN_COLS = sum(IN_SIZES)

kernel_name = "hybrid_shortconv_retention_moe_dit_step"


def rmsnorm(x, g):
    xf = x.astype(jnp.float32)
    y = xf * lax.rsqrt(jnp.mean(xf * xf, axis=-1, keepdims=True) + EPS)
    return y.astype(x.dtype) * g


def modulation(cvec, w_ada, b_ada):
    m = jax.nn.silu(cvec) @ w_ada + b_ada
    return jnp.split(m[:, None, :], N_MOD, axis=-1)


def depthwise_conv3(u, w, b):
    pad = [(0, 0)] * (u.ndim - 2) + [(1, 1), (0, 0)]
    up = jnp.pad(u, pad)
    return up[..., :-2, :] * w[0] + up[..., 1:-1, :] * w[1] + up[..., 2:, :] * w[2] + b


def rope_half(x, ang):
    x1, x2 = jnp.split(x, 2, axis=-1)
    cos = jnp.cos(ang)[None, :, None, :].astype(x.dtype)
    sin = jnp.sin(ang)[None, :, None, :].astype(x.dtype)
    return jnp.concatenate([x1 * cos - x2 * sin, x1 * sin + x2 * cos], axis=-1)


def rope2d(x):
    t = x.shape[1]
    pos = jnp.arange(t)
    row = (pos // GRID_W).astype(jnp.float32)
    col = (pos % GRID_W).astype(jnp.float32)
    n_freq = RET_DK // 4
    freqs = ROPE_THETA ** (-jnp.arange(n_freq, dtype=jnp.float32) / n_freq)
    xr, xc = jnp.split(x, 2, axis=-1)
    return jnp.concatenate([rope_half(xr, row[:, None] * freqs), rope_half(xc, col[:, None] * freqs)], axis=-1)


def retention_chunked(q, k, v, log_gamma, s0, strict):
    b, t, h, _ = q.shape
    dv = v.shape[-1]
    n = t // CHUNK

    def to_chunks(a):
        return a.astype(jnp.float32).reshape(b, n, CHUNK, h, a.shape[-1]).transpose(1, 0, 3, 2, 4)

    qc, kc, vc = to_chunks(q), to_chunks(k), to_chunks(v)
    idx = jnp.arange(CHUNK, dtype=jnp.float32)
    diff = idx[:, None] - idx[None, :]
    mask = (diff > 0) if strict else (diff >= 0)
    decay_in = jnp.where(mask[None], jnp.exp(jnp.maximum(diff, 0.0)[None] * log_gamma[:, None, None]), 0.0)
    decay_q = jnp.exp((idx + 1.0)[None, :] * log_gamma[:, None])[..., None]
    decay_k = jnp.exp((CHUNK - 1.0 - idx)[None, :] * log_gamma[:, None])[..., None]
    decay_chunk = jnp.exp(CHUNK * log_gamma)[:, None, None]

    def step(s, blk):
        qb, kb, vb = blk
        scores = jnp.einsum('bhid,bhjd->bhij', qb, kb) * decay_in
        o = jnp.einsum('bhij,bhje->bhie', scores, vb) + jnp.einsum('bhid,bhde->bhie', qb, s) * decay_q
        s_new = decay_chunk * s + jnp.einsum('bhjd,bhje->bhde', kb * decay_k, vb)
        return s_new, o

    s_fin, oc = lax.scan(step, s0.astype(jnp.float32), (qc, kc, vc))
    o = oc.transpose(1, 0, 3, 2, 4).reshape(b, t, h, dv)
    return o, s_fin


def token_mixers(h, w_in, conv_w, conv_b, decay_logit, gn_w, w_br_conv, w_br_ret, w_out, latent, s0):
    b, t, _ = h.shape
    proj = h @ w_in
    cx, cb, cc, q, k, v, g, mg_conv, mg_ret = jnp.split(proj, list(np.cumsum(IN_SIZES)[:-1]), axis=-1)

    u = cc * cx
    if latent:
        rows = t // GRID_W
        u = u.reshape(b, rows, GRID_W, CONV_WIDTH)
    uc = depthwise_conv3(u, conv_w, conv_b).reshape(b, t, CONV_WIDTH)
    y_conv = (cb * uc) @ w_br_conv

    q = q.reshape(b, t, RET_HEADS, RET_DK)
    k = k.reshape(b, t, RET_HEADS, RET_DK) * (RET_DK ** -0.5)
    v = v.reshape(b, t, RET_HEADS, RET_DV)
    if latent:
        q, k = rope2d(q), rope2d(k)
    log_gamma = jax.nn.log_sigmoid(decay_logit.astype(jnp.float32))
    o_f, s_f = retention_chunked(q, k, v, log_gamma[0], s0[:, 0], False)
    o_b, s_b = retention_chunked(jnp.flip(q, 1), jnp.flip(k, 1), jnp.flip(v, 1), log_gamma[1], s0[:, 1], True)
    o = o_f + jnp.flip(o_b, 1)
    mu = jnp.mean(o, axis=-1, keepdims=True)
    var = jnp.mean(jnp.square(o - mu), axis=-1, keepdims=True)
    o = ((o - mu) * lax.rsqrt(var + EPS)).reshape(b, t, RET_V).astype(h.dtype) * gn_w
    y_ret = (jax.nn.silu(g) * o) @ w_br_ret

    out = (jax.nn.sigmoid(mg_conv) * y_conv + jax.nn.sigmoid(mg_ret) * y_ret) @ w_out
    return out, jnp.stack([s_f, s_b], axis=1)


def moe_ffn(x, rw, rb, wg, bg, wu, bu, wd, bd):
    logits = (x @ rw + rb).astype(jnp.float32)
    top_v, top_i = lax.top_k(logits, TOP_K)
    top_w = jax.nn.softmax(top_v, axis=-1)
    combine = jnp.einsum('tk,tke->te', top_w, jax.nn.one_hot(top_i, N_EXPERTS, dtype=jnp.float32)).astype(x.dtype)
    y = jnp.zeros_like(x)
    for e in range(N_EXPERTS):
        gate = jnp.minimum(x @ wg[e] + bg[e], SWIGLU_LIMIT)
        up = jnp.clip(x @ wu[e] + bu[e], -SWIGLU_LIMIT, SWIGLU_LIMIT)
        act = gate * jax.nn.sigmoid(SWIGLU_ALPHA * gate) * (up + 1.0)
        y = y + combine[:, e:e + 1] * (act @ wd[e] + bd[e])
    return y


def setup_inputs(seed: int = 0) -> dict:
    key = jax.random.key(seed)
    ks = jax.random.split(key, 32)
    f32 = jnp.float32
    D = D_MODEL

    def nrm(k, shape, scale):
        return jax.random.normal(k, shape, f32) * scale

    heads = jnp.arange(RET_HEADS, dtype=f32)
    base_logit = (5.0 + heads) * np.float32(np.log(2.0))
    return {
        "x_prompt": nrm(ks[0], (BATCH, SEQ, D), 1.0),
        "x_sample": nrm(ks[1], (DEC_BATCH, DEC_SEQ, D), 1.0),
        "state_ret": nrm(ks[2], (DEC_BATCH, DEPTH, 2, RET_HEADS, RET_DK, RET_DV), 0.5),
        "c": nrm(ks[3], (DEC_BATCH, D), 1.0),
        "c_ctx": nrm(ks[4], (D,), 1.0),
        "w_ada": nrm(ks[5], (DEPTH, D, N_MOD * D), D ** -0.5),
        "b_ada": nrm(ks[6], (DEPTH, N_MOD * D), 0.02),
        "norm_mix_pre": 1.0 + nrm(ks[7], (DEPTH, D), 0.02),
        "norm_mix_post": 1.0 + nrm(ks[8], (DEPTH, D), 0.02),
        "norm_ffn_pre": 1.0 + nrm(ks[9], (DEPTH, D), 0.02),
        "norm_ffn_post": 1.0 + nrm(ks[10], (DEPTH, D), 0.02),
        "w_in": nrm(ks[11], (DEPTH, D, IN_COLS), D ** -0.5),
        "conv_w": nrm(ks[12], (DEPTH, CONV_K, CONV_WIDTH), CONV_K ** -0.5),
        "conv_b": nrm(ks[13], (DEPTH, CONV_WIDTH), 0.02),
        "ret_decay_logit": base_logit[None, None, :] + nrm(ks[14], (DEPTH, 2, RET_HEADS), 0.05),
        "ret_gn_w": 1.0 + nrm(ks[15], (DEPTH, RET_V), 0.02),
        "w_br_conv": nrm(ks[16], (DEPTH, CONV_WIDTH, D), CONV_WIDTH ** -0.5),
        "w_br_ret": nrm(ks[17], (DEPTH, RET_V, D), RET_V ** -0.5),
        "w_out": nrm(ks[18], (DEPTH, D, D), D ** -0.5),
        "router_w": nrm(ks[19], (DEPTH, D, N_EXPERTS), D ** -0.5),
        "router_b": nrm(ks[20], (DEPTH, N_EXPERTS), 0.01),
        "moe_w_gate": nrm(ks[21], (DEPTH, N_EXPERTS, D, D_FF), D ** -0.5),
        "moe_b_gate": nrm(ks[22], (DEPTH, N_EXPERTS, D_FF), 0.02),
        "moe_w_up": nrm(ks[23], (DEPTH, N_EXPERTS, D, D_FF), D ** -0.5),
        "moe_b_up": nrm(ks[24], (DEPTH, N_EXPERTS, D_FF), 0.02),
        "moe_w_down": nrm(ks[25], (DEPTH, N_EXPERTS, D_FF, D), D_FF ** -0.5),
        "moe_b_down": nrm(ks[26], (DEPTH, N_EXPERTS, D), 0.02),
    }


def reference(x_prompt, x_sample, state_ret, c, c_ctx, w_ada, b_ada, norm_mix_pre, norm_mix_post,
              norm_ffn_pre, norm_ffn_post, w_in, conv_w, conv_b, ret_decay_logit, ret_gn_w,
              w_br_conv, w_br_ret, w_out, router_w, router_b, moe_w_gate, moe_b_gate,
              moe_w_up, moe_b_up, moe_w_down, moe_b_down):
    xp, xs = x_prompt, x_sample
    bp, tp, _ = xp.shape
    new_states = []
    for l in range(DEPTH):
        sh1p, sc1p, g1p, sh2p, sc2p, g2p = modulation(c_ctx[None, :], w_ada[l], b_ada[l])
        sh1s, sc1s, g1s, sh2s, sc2s, g2s = modulation(c, w_ada[l], b_ada[l])

        hp = rmsnorm(xp, norm_mix_pre[l]) * (1.0 + sc1p) + sh1p
        hs = rmsnorm(xs, norm_mix_pre[l]) * (1.0 + sc1s) + sh1s
        s_zero = jnp.zeros((bp, 2, RET_HEADS, RET_DK, RET_DV), jnp.float32)
        mp, sp = token_mixers(hp, w_in[l], conv_w[l], conv_b[l], ret_decay_logit[l], ret_gn_w[l],
                              w_br_conv[l], w_br_ret[l], w_out[l], False, s_zero)
        ms, _ = token_mixers(hs, w_in[l], conv_w[l], conv_b[l], ret_decay_logit[l], ret_gn_w[l],
                             w_br_conv[l], w_br_ret[l], w_out[l], True, state_ret[:, l])
        new_states.append(sp)
        xp = xp + g1p * rmsnorm(mp, norm_mix_post[l])
        xs = xs + g1s * rmsnorm(ms, norm_mix_post[l])

        hp = rmsnorm(xp, norm_ffn_pre[l]) * (1.0 + sc2p) + sh2p
        hs = rmsnorm(xs, norm_ffn_pre[l]) * (1.0 + sc2s) + sh2s
        n_ctx = bp * tp
        tokens = jnp.concatenate([hp.reshape(-1, D_MODEL), hs.reshape(-1, D_MODEL)], axis=0)
        f = moe_ffn(tokens, router_w[l], router_b[l], moe_w_gate[l], moe_b_gate[l],
                    moe_w_up[l], moe_b_up[l], moe_w_down[l], moe_b_down[l])
        fp = f[:n_ctx].reshape(xp.shape)
        fs = f[n_ctx:].reshape(xs.shape)
        xp = xp + g2p * rmsnorm(fp, norm_ffn_post[l])
        xs = xs + g2s * rmsnorm(fs, norm_ffn_post[l])

    new_state_ret = jnp.stack(new_states, axis=1)
    return (xp, xs, new_state_ret)
```

```python
import functools

import jax
import jax.numpy as jnp
import numpy as np
from jax import lax
from jax.experimental import pallas as pl
from jax.experimental.pallas import tpu as pltpu

F32 = jnp.float32
BF16 = jnp.bfloat16
I32 = jnp.int32

D = 2048
BATCH = 16
SEQ = 256
DEC_BATCH = 2
DEC_SEQ = 1024
GRID_W = 64
CONV_W = 1024
HEADS = 8
DK = 128
DV = 256
CHUNK = 128
N_EXP = 32
TOP_K = 4
D_FF = 2048
LIMIT = 7.0
ALPHA = 1.702
THETA = 10000.0
EPS = 1e-6
N_MOD = 6
N_CTX = BATCH * SEQ
N_LAT = DEC_BATCH * DEC_SEQ
N_TOK = N_CTX + N_LAT
IN_COLS = 3 * CONV_W + 2 * HEADS * DK + 2 * HEADS * DV + 2 * D
COL_CX, COL_CB, COL_CC = 0, CONV_W, 2 * CONV_W
COL_Q = 3 * CONV_W
COL_K = COL_Q + HEADS * DK
COL_V = COL_K + HEADS * DK
COL_G = COL_V + HEADS * DV
COL_MGC = COL_G + HEADS * DV
COL_MGR = COL_MGC + D
MOD_SH1, MOD_SC1, MOD_G1, MOD_SH2, MOD_SC2, MOD_G2 = range(6)
MOD_ROWS = 8

MOE_TM = 256
N_PAIR = N_TOK * TOP_K
MOE_TILES = (N_PAIR + N_EXP * (MOE_TM - 1)) // MOE_TM + 1
MOE_ROWS = MOE_TILES * MOE_TM
MOE_JN = 1024

MIB = 1024 * 1024


def _cparams(sem, vmem_mib):
    return pltpu.CompilerParams(dimension_semantics=sem, vmem_limit_bytes=vmem_mib * MIB)


def _mod_row(t0):
    return jnp.where(t0 < N_CTX, 0, 1 + (t0 - N_CTX) // DEC_SEQ)


def _mod_spec(tm, chunk):
    return pl.BlockSpec((None, 1, D), lambda i: (_mod_row(i * tm), 0, chunk))


def _rms(x, g):
    return (x * lax.rsqrt(jnp.mean(x * x, axis=-1, keepdims=True) + EPS)) * g


def _ada_kernel(c_ref, w_ref, b_ref, o_ref):
    c = c_ref[...]
    s = c * jax.nn.sigmoid(c)
    o_ref[...] = jnp.dot(s.astype(BF16), w_ref[...].astype(BF16),
                         preferred_element_type=F32) + b_ref[...]


def _modulation(cvec, w_ada, b_ada):
    tn = 1024
    return pl.pallas_call(
        _ada_kernel,
        grid=(N_MOD * D // tn,),
        in_specs=[pl.BlockSpec((MOD_ROWS, D), lambda j: (0, 0)),
                  pl.BlockSpec((D, tn), lambda j: (0, j)),
                  pl.BlockSpec((1, tn), lambda j: (0, j))],
        out_specs=pl.BlockSpec((MOD_ROWS, tn), lambda j: (0, j)),
        out_shape=jax.ShapeDtypeStruct((MOD_ROWS, N_MOD * D), F32),
        compiler_params=_cparams(("arbitrary",), 40),
        name="ada_modulation",
    )(cvec, w_ada, b_ada.reshape(1, -1))


def _norm_mod_kernel(x_ref, g_ref, sh_ref, sc_ref, o_ref):
    y = _rms(x_ref[...], g_ref[...])
    o_ref[...] = (y * (1.0 + sc_ref[...]) + sh_ref[...]).astype(BF16)


def _norm_mod(x, g, mod3):
    tm = 512
    return pl.pallas_call(
        _norm_mod_kernel,
        grid=(N_TOK // tm,),
        in_specs=[pl.BlockSpec((tm, D), lambda i: (i, 0)),
                  pl.BlockSpec((1, D), lambda i: (0, 0)),
                  _mod_spec(tm, MOD_SH1),
                  _mod_spec(tm, MOD_SC1)],
        out_specs=pl.BlockSpec((tm, D), lambda i: (i, 0)),
        out_shape=jax.ShapeDtypeStruct((N_TOK, D), BF16),
        compiler_params=_cparams(("arbitrary",), 40),
        name="norm_modulate",
    )(x, g.reshape(1, D), mod3, mod3)


def _proj_kernel(h_ref, w_ref, o_ref, wb_ref):
    @pl.when(pl.program_id(1) == 0)
    def _():
        wb_ref[...] = w_ref[...].astype(BF16)

    o_ref[...] = jnp.dot(h_ref[...], wb_ref[...], preferred_element_type=F32).astype(BF16)


def _in_proj(h, w_in):
    tm, tn = 1024, 1024
    return pl.pallas_call(
        _proj_kernel,
        grid=(IN_COLS // tn, N_TOK // tm),
        in_specs=[pl.BlockSpec((tm, D), lambda n, m: (m, 0)),
                  pl.BlockSpec((D, tn), lambda n, m: (0, n))],
        out_specs=pl.BlockSpec((tm, tn), lambda n, m: (m, n)),
        out_shape=jax.ShapeDtypeStruct((N_TOK, IN_COLS), BF16),
        scratch_shapes=[pltpu.VMEM((D, tn), BF16)],
        compiler_params=_cparams(("arbitrary", "arbitrary"), 48),
        name="in_proj",
    )(h, w_in)


CONV_TM = 256


def _conv_kernel(cx_ref, cb_ref, cc_ref, mg0_ref, mg1_ref, cw_ref, cbias_ref, w_ref, o_ref):
    i = pl.program_id(0)
    u = cc_ref[...].astype(F32) * cx_ref[...].astype(F32)
    seg = jnp.where(i * CONV_TM < N_CTX, SEQ, GRID_W)
    pos = lax.broadcasted_iota(I32, (CONV_TM, 1), 0) & (seg - 1)
    prev = jnp.where(pos == 0, 0.0, pltpu.roll(u, 1, 0))
    nxt = jnp.where(pos == seg - 1, 0.0, pltpu.roll(u, CONV_TM - 1, 0))
    cw = cw_ref[...]
    uc = prev * cw[0:1, :] + u * cw[1:2, :] + nxt * cw[2:3, :] + cbias_ref[...]
    z = (cb_ref[...].astype(F32) * uc).astype(BF16)
    y = jnp.dot(z, w_ref[...], preferred_element_type=F32)
    mg = jnp.concatenate([mg0_ref[...], mg1_ref[...]], axis=1).astype(F32)
    o_ref[...] = (jax.nn.sigmoid(mg) * y).astype(BF16)


def _conv_branch(proj, conv_w, conv_b, w_br_conv_bf):
    tm = CONV_TM
    cw = 1024

    def col(c):
        return pl.BlockSpec((tm, cw), lambda i: (i, c))

    return pl.pallas_call(
        _conv_kernel,
        grid=(N_TOK // tm,),
        in_specs=[col(COL_CX // cw), col(COL_CB // cw), col(COL_CC // cw),
                  col(COL_MGC // cw), col(COL_MGC // cw + 1),
                  pl.BlockSpec((3, CONV_W), lambda i: (0, 0)),
                  pl.BlockSpec((1, CONV_W), lambda i: (0, 0)),
                  pl.BlockSpec((CONV_W, D), lambda i: (0, 0))],
        out_specs=pl.BlockSpec((tm, D), lambda i: (i, 0)),
        out_shape=jax.ShapeDtypeStruct((N_TOK, D), BF16),
        compiler_params=_cparams(("arbitrary",), 48),
        name="conv_branch",
    )(proj, proj, proj, proj, proj, conv_w, conv_b.reshape(1, CONV_W), w_br_conv_bf)


def _log_sigmoid(x):
    return -(jnp.maximum(-x, 0.0) + jnp.log1p(jnp.exp(-jnp.abs(x))))


def _rope(x, cos, sin_signed):
    lane = lax.broadcasted_iota(I32, x.shape, 1)
    swapped = jnp.where((lane & 63) < 32, pltpu.roll(x, 96, 1), pltpu.roll(x, 32, 1))
    return x * cos + swapped * sin_signed


def _retention_kernel(*refs, seq_len, latent):
    if latent:
        (dl_ref, q_ref, k_ref, v_ref, g_ref, gn_ref, cos_ref, sin_ref, s0_ref,
         r_ref, o_acc) = refs
    else:
        dl_ref, q_ref, k_ref, v_ref, g_ref, gn_ref, r_ref, sout_ref, o_acc = refs
    h = pl.program_id(1)
    c = CHUNK
    n_chunks = seq_len // c

    lgf = _log_sigmoid(jnp.full((c, c), dl_ref[0, h], F32))
    lgb = _log_sigmoid(jnp.full((c, c), dl_ref[1, h], F32))
    lgf_w = _log_sigmoid(jnp.full((c, DV), dl_ref[0, h], F32))
    lgb_w = _log_sigmoid(jnp.full((c, DV), dl_ref[1, h], F32))

    ii = lax.broadcasted_iota(I32, (c, c), 0).astype(F32)
    jj = lax.broadcasted_iota(I32, (c, c), 1).astype(F32)
    diff = ii - jj
    m_in = jnp.where(diff >= 0.0, jnp.exp(jnp.maximum(diff, 0.0) * lgf),
                     jnp.exp(jnp.maximum(-diff, 0.0) * lgb))
    io = lax.broadcasted_iota(I32, (c, DV), 0).astype(F32)
    dq_f = jnp.exp((io + 1.0) * lgf_w)
    dq_b = jnp.exp((c - io) * lgb_w)
    dk_f = jnp.exp((c - 1.0 - jj) * lgf)
    dk_b = jnp.exp(jj * lgb)
    dc_f = jnp.exp(c * lgf_w)
    dc_b = jnp.exp(c * lgb_w)

    q = q_ref[...].astype(F32)
    k = k_ref[...].astype(F32) * (DK ** -0.5)
    if latent:
        q = _rope(q, cos_ref[...], sin_ref[...])
        k = _rope(k, cos_ref[...], sin_ref[...])
        s_f = s0_ref[0]
        s_b = s0_ref[1]
    else:
        s_f = jnp.zeros((DK, DV), F32)
        s_b = jnp.zeros((DK, DV), F32)
    qb = q.astype(BF16)

    def chunk(x, n):
        return x[n * c:(n + 1) * c]

    for n in range(n_chunks):
        qn = chunk(qb, n)
        kt = chunk(k, n).T
        vn = v_ref[n * c:(n + 1) * c, :]
        s = jnp.dot(qn, kt.astype(BF16), preferred_element_type=F32)
        p = (s * m_in).astype(BF16)
        o = jnp.dot(p, vn, preferred_element_type=F32)
        o = o + jnp.dot(qn, s_f.astype(BF16), preferred_element_type=F32) * dq_f
        o_acc[n * c:(n + 1) * c, :] = o
        s_f = dc_f * s_f + jnp.dot((kt * dk_f).astype(BF16), vn, preferred_element_type=F32)

    for n in reversed(range(n_chunks)):
        qn = chunk(qb, n)
        kt = chunk(k, n).T
        vn = v_ref[n * c:(n + 1) * c, :]
        o_acc[n * c:(n + 1) * c, :] += (
            jnp.dot(qn, s_b.astype(BF16), preferred_element_type=F32) * dq_b)
        s_b = dc_b * s_b + jnp.dot((kt * dk_b).astype(BF16), vn, preferred_element_type=F32)

    if not latent:
        sout_ref[0] = s_f
        sout_ref[1] = s_b

    o = o_acc[...]
    mu = jnp.mean(o, axis=-1, keepdims=True)
    oc = o - mu
    var = jnp.mean(oc * oc, axis=-1, keepdims=True)
    on = oc * lax.rsqrt(var + EPS) * gn_ref[...]
    g = g_ref[...].astype(F32)
    r_ref[...] = ((g * jax.nn.sigmoid(g)) * on).astype(BF16)


def _retention(proj, decay_logit, gn_w, cos_t, sin_t, state0, *, latent):
    seq_len = DEC_SEQ if latent else SEQ
    nb = DEC_BATCH if latent else BATCH
    rb0 = (N_CTX // DEC_SEQ) if latent else 0

    def spec(width, col0):
        return pl.BlockSpec((seq_len, width), lambda b, h, dl: (rb0 + b, col0 // width + h))

    in_specs = [spec(DK, COL_Q), spec(DK, COL_K), spec(DV, COL_V), spec(DV, COL_G),
                pl.BlockSpec((1, DV), lambda b, h, dl: (0, h))]
    args = [proj, proj, proj, proj, gn_w.reshape(1, HEADS * DV)]
    state_spec = pl.BlockSpec((None, None, 2, None, DK, DV), lambda b, h, dl: (b, 0, 0, h, 0, 0))
    r_shape = jax.ShapeDtypeStruct((nb * seq_len, HEADS * DV), BF16)
    r_spec = pl.BlockSpec((seq_len, DV), lambda b, h, dl: (b, h))
    if latent:
        in_specs += [pl.BlockSpec((seq_len, DK), lambda b, h, dl: (0, 0)),
                     pl.BlockSpec((seq_len, DK), lambda b, h, dl: (0, 0)),
                     state_spec]
        args += [cos_t, sin_t, state0]
        out_specs, out_shape = r_spec, r_shape
    else:
        out_specs = (r_spec, state_spec)
        out_shape = (r_shape, jax.ShapeDtypeStruct((BATCH, 1, 2, HEADS, DK, DV), F32))
    return pl.pallas_call(
        functools.partial(_retention_kernel, seq_len=seq_len, latent=latent),
        grid_spec=pltpu.PrefetchScalarGridSpec(
            num_scalar_prefetch=1, grid=(nb, HEADS), in_specs=in_specs, out_specs=out_specs,
            scratch_shapes=[pltpu.VMEM((seq_len, DV), F32)]),
        out_shape=out_shape,
        compiler_params=_cparams(("arbitrary", "arbitrary"), 40),
        name="retention_latent" if latent else "retention_context",
    )(decay_logit, *args)


MIX_TM = 256


def _mix_kernel(rc_ref, rl_ref, wbr_ref, a_ref, mg0_ref, mg1_ref, wout_ref, x_ref, g1_ref,
                sc2_ref, sh2_ref, npost_ref, npre_ref, rwt_ref, rb_ref,
                x1_ref, h2_ref, ti_ref, tw_ref):
    is_ctx = pl.program_id(0) * MIX_TM < N_CTX
    r = jnp.where(is_ctx, rc_ref[...], rl_ref[...])
    y = jnp.dot(r, wbr_ref[...], preferred_element_type=F32)
    mg = jnp.concatenate([mg0_ref[...], mg1_ref[...]], axis=1).astype(F32)
    mix = a_ref[...].astype(F32) + jax.nn.sigmoid(mg) * y
    out = jnp.dot(mix.astype(BF16), wout_ref[...], preferred_element_type=F32)
    x1 = x_ref[...] + g1_ref[...] * _rms(out, npost_ref[...])
    x1_ref[...] = x1
    h2 = _rms(x1, npre_ref[...]) * (1.0 + sc2_ref[...]) + sh2_ref[...]
    h2_ref[...] = h2
    logits = lax.dot_general(rwt_ref[...], h2.astype(BF16), (((1,), (1,)), ((), ())),
                             preferred_element_type=F32) + rb_ref[...]
    eidx = lax.broadcasted_iota(I32, logits.shape, 0).astype(F32)
    vals, idxs = [], []
    for _ in range(TOP_K):
        m = jnp.max(logits, axis=0, keepdims=True)
        am = jnp.min(jnp.where(logits == m, eidx, float(N_EXP)), axis=0, keepdims=True)
        vals.append(m)
        idxs.append(am)
        logits = jnp.where(eidx == am, -jnp.inf, logits)
    es = [jnp.exp(v - vals[0]) for v in vals]
    denom = es[0] + es[1] + es[2] + es[3]
    for kk in range(TOP_K):
        ti_ref[kk:kk + 1, :] = idxs[kk].astype(I32)
        tw_ref[kk:kk + 1, :] = es[kk] / denom


def _mix_route(r_ctx, r_lat, wbr_bf, a, proj, wout_bf, x, mod3, npost, npre, rwt_bf, rb):
    tm = MIX_TM
    cw = 1024
    ctx_tiles = N_CTX // tm

    def full(shape):
        return pl.BlockSpec(shape, lambda i: (0,) * len(shape), pipeline_mode=pl.Buffered(1))

    def row(width):
        return pl.BlockSpec((tm, width), lambda i: (i, 0))

    return pl.pallas_call(
        _mix_kernel,
        grid=(N_TOK // tm,),
        in_specs=[pl.BlockSpec((tm, D), lambda i: (jnp.minimum(i, ctx_tiles - 1), 0)),
                  pl.BlockSpec((tm, D), lambda i: (jnp.maximum(i - ctx_tiles, 0), 0)),
                  full((D, D)), row(D),
                  pl.BlockSpec((tm, cw), lambda i: (i, COL_MGR // cw)),
                  pl.BlockSpec((tm, cw), lambda i: (i, COL_MGR // cw + 1)),
                  full((D, D)), row(D),
                  _mod_spec(tm, MOD_G1), _mod_spec(tm, MOD_SC2), _mod_spec(tm, MOD_SH2),
                  full((1, D)), full((1, D)), full((N_EXP, D)), full((N_EXP, 1))],
        out_specs=(row(D), row(D),
                   pl.BlockSpec((TOP_K, tm), lambda i: (0, i)),
                   pl.BlockSpec((TOP_K, tm), lambda i: (0, i))),
        out_shape=(jax.ShapeDtypeStruct((N_TOK, D), F32),
                   jax.ShapeDtypeStruct((N_TOK, D), F32),
                   jax.ShapeDtypeStruct((TOP_K, N_TOK), I32),
                   jax.ShapeDtypeStruct((TOP_K, N_TOK), F32)),
        compiler_params=_cparams(("arbitrary",), 56),
        name="mix_out_route",
    )(r_ctx, r_lat, wbr_bf, a, proj, proj, wout_bf, x, mod3, mod3, mod3,
      npost.reshape(1, D), npre.reshape(1, D), rwt_bf, rb.reshape(N_EXP, 1))


def _routing_tables(top_i):
    e = top_i.reshape(-1)
    onehot = (e[:, None] == jnp.arange(N_EXP, dtype=I32)[None, :]).astype(I32)
    csum = jnp.cumsum(onehot, axis=0)
    rank = jnp.take_along_axis(csum, e[:, None], axis=1)[:, 0] - 1
    counts = csum[-1]
    tiles_e = (counts + MOE_TM - 1) // MOE_TM
    tile_end = jnp.cumsum(tiles_e)
    tile_start = tile_end - tiles_e
    slot = tile_start[e] * MOE_TM + rank
    n_tiles = tile_end[-1]
    tok = jnp.tile(jnp.arange(N_TOK, dtype=I32), TOP_K)
    tok_of_slot = jnp.zeros((MOE_ROWS,), I32).at[slot].set(tok)
    tile_ids = jnp.arange(MOE_TILES, dtype=I32)
    last = jnp.maximum(n_tiles - 1, 0)
    tile_blk = jnp.minimum(tile_ids, last)
    tile_exp = jnp.sum((tile_blk[:, None] >= tile_end[None, :]).astype(I32), axis=1)
    tile_exp = jnp.minimum(tile_exp, N_EXP - 1)
    return slot, tok_of_slot, tile_blk, tile_exp, n_tiles.reshape(1).astype(I32)


def _row_copy(src_hbm, row, dst_ref, r, sem):
    return pltpu.make_async_copy(src_hbm.at[pl.ds(row, 1), :], dst_ref.at[pl.ds(r, 1), :], sem)


def _dispatch_kernel(tos_ref, blk_ref, nt_ref, x_hbm, o_ref, sem):
    j = pl.program_id(0)

    @pl.when(j < nt_ref[0])
    def _():
        base = j * MOE_TM

        def issue(r, carry):
            _row_copy(x_hbm, tos_ref[base + r], o_ref, r, sem).start()
            return carry

        lax.fori_loop(0, MOE_TM, issue, 0, unroll=8)

        def wait(r, carry):
            _row_copy(x_hbm, 0, o_ref, r, sem).wait()
            return carry

        lax.fori_loop(0, MOE_TM, wait, 0, unroll=8)

    @pl.when(j >= nt_ref[0])
    def _():
        o_ref[...] = jnp.zeros_like(o_ref)


def _dispatch(h2, tok_of_slot, tile_blk, n_tiles):
    return pl.pallas_call(
        _dispatch_kernel,
        grid_spec=pltpu.PrefetchScalarGridSpec(
            num_scalar_prefetch=3, grid=(MOE_TILES,),
            in_specs=[pl.BlockSpec(memory_space=pl.ANY)],
            out_specs=pl.BlockSpec((MOE_TM, D), lambda j, tos, blk, nt: (j, 0)),
            scratch_shapes=[pltpu.SemaphoreType.DMA(())]),
        out_shape=jax.ShapeDtypeStruct((MOE_ROWS, D), F32),
        compiler_params=_cparams(("arbitrary",), 32),
        name="moe_dispatch",
    )(tok_of_slot, tile_blk, n_tiles, h2)


def _expert_changed(t, exp_ref):
    prev = exp_ref[jnp.maximum(t - 1, 0)]
    return jnp.logical_or(t == 0, exp_ref[t] != prev)


def _moe_up_kernel(blk_ref, exp_ref, nt_ref, x_ref, wg_ref, wu_ref, bg_ref, bu_ref, o_ref,
                   wgb_ref, wub_ref):
    t = pl.program_id(1)

    @pl.when(_expert_changed(t, exp_ref))
    def _():
        wgb_ref[...] = wg_ref[...].astype(BF16)
        wub_ref[...] = wu_ref[...].astype(BF16)

    @pl.when(t < nt_ref[0])
    def _():
        x = x_ref[...].astype(BF16)
        gate = jnp.dot(x, wgb_ref[...], preferred_element_type=F32) + bg_ref[...]
        up = jnp.dot(x, wub_ref[...], preferred_element_type=F32) + bu_ref[...]
        gate = jnp.minimum(gate, LIMIT)
        up = jnp.clip(up, -LIMIT, LIMIT)
        o_ref[...] = (gate * jax.nn.sigmoid(ALPHA * gate) * (up + 1.0)).astype(BF16)

    @pl.when(t >= nt_ref[0])
    def _():
        o_ref[...] = jnp.zeros_like(o_ref)


def _moe_up(xs, wg, wu, bg, bu, tile_blk, tile_exp, n_tiles):
    tn = MOE_JN
    w_spec = pl.BlockSpec((None, D, tn), lambda j, t, blk, ex, nt: (ex[t], 0, j))
    b_spec = pl.BlockSpec((None, 1, tn), lambda j, t, blk, ex, nt: (ex[t], 0, j))
    return pl.pallas_call(
        _moe_up_kernel,
        grid_spec=pltpu.PrefetchScalarGridSpec(
            num_scalar_prefetch=3, grid=(D_FF // tn, MOE_TILES),
            in_specs=[pl.BlockSpec((MOE_TM, D), lambda j, t, blk, ex, nt: (blk[t], 0)),
                      w_spec, w_spec, b_spec, b_spec],
            out_specs=pl.BlockSpec((MOE_TM, tn), lambda j, t, blk, ex, nt: (t, j)),
            scratch_shapes=[pltpu.VMEM((D, tn), BF16), pltpu.VMEM((D, tn), BF16)]),
        out_shape=jax.ShapeDtypeStruct((MOE_ROWS, D_FF), BF16),
        compiler_params=_cparams(("arbitrary", "arbitrary"), 58),
        name="moe_gate_up",
    )(tile_blk, tile_exp, n_tiles, xs, wg, wu,
      bg.reshape(N_EXP, 1, D_FF), bu.reshape(N_EXP, 1, D_FF))


def _moe_down_kernel(blk_ref, exp_ref, nt_ref, h_ref, wd_ref, bd_ref, o_ref, wdb_ref):
    t = pl.program_id(1)

    @pl.when(_expert_changed(t, exp_ref))
    def _():
        wdb_ref[...] = wd_ref[...].astype(BF16)

    @pl.when(t < nt_ref[0])
    def _():
        o_ref[...] = jnp.dot(h_ref[...], wdb_ref[...], preferred_element_type=F32) + bd_ref[...]

    @pl.when(t >= nt_ref[0])
    def _():
        o_ref[...] = jnp.zeros_like(o_ref)


def _moe_down(hact, wd, bd, tile_blk, tile_exp, n_tiles):
    tn = MOE_JN
    return pl.pallas_call(
        _moe_down_kernel,
        grid_spec=pltpu.PrefetchScalarGridSpec(
            num_scalar_prefetch=3, grid=(D // tn, MOE_TILES),
            in_specs=[pl.BlockSpec((MOE_TM, D_FF), lambda j, t, blk, ex, nt: (blk[t], 0)),
                      pl.BlockSpec((None, D_FF, tn), lambda j, t, blk, ex, nt: (ex[t], 0, j)),
                      pl.BlockSpec((None, 1, tn), lambda j, t, blk, ex, nt: (ex[t], 0, j))],
            out_specs=pl.BlockSpec((MOE_TM, tn), lambda j, t, blk, ex, nt: (t, j)),
            scratch_shapes=[pltpu.VMEM((D_FF, tn), BF16)]),
        out_shape=jax.ShapeDtypeStruct((MOE_ROWS, D), F32),
        compiler_params=_cparams(("arbitrary", "arbitrary"), 48),
        name="moe_down",
    )(tile_blk, tile_exp, n_tiles, hact, wd, bd.reshape(N_EXP, 1, D))


COMB_TM = 256


def _combine_kernel(slot_ref, ys_hbm, w_ref, x1_ref, g2_ref, npost_ref, o_ref, buf, sem):
    i = pl.program_id(0)
    base = i * COMB_TM
    for kk in range(TOP_K):
        def issue(r, carry, kk=kk):
            _row_copy(ys_hbm, slot_ref[kk * N_TOK + base + r], buf.at[kk], r, sem).start()
            return carry

        lax.fori_loop(0, COMB_TM, issue, 0, unroll=8)
    for kk in range(TOP_K):
        def wait(r, carry, kk=kk):
            _row_copy(ys_hbm, 0, buf.at[kk], r, sem).wait()
            return carry

        lax.fori_loop(0, COMB_TM, wait, 0, unroll=8)
    w = w_ref[...]
    f = buf[0] * w[:, 0:1]
    for kk in range(1, TOP_K):
        f = f + buf[kk] * w[:, kk:kk + 1]
    o_ref[...] = x1_ref[...] + g2_ref[...] * _rms(f, npost_ref[...])


def _combine(ys, slot, top_w_t, x1, mod3, npost):
    tm = COMB_TM
    return pl.pallas_call(
        _combine_kernel,
        grid_spec=pltpu.PrefetchScalarGridSpec(
            num_scalar_prefetch=1, grid=(N_TOK // tm,),
            in_specs=[pl.BlockSpec(memory_space=pl.ANY),
                      pl.BlockSpec((tm, TOP_K), lambda i, s: (i, 0)),
                      pl.BlockSpec((tm, D), lambda i, s: (i, 0)),
                      pl.BlockSpec((None, 1, D), lambda i, s: (_mod_row(i * tm), 0, MOD_G2)),
                      pl.BlockSpec((1, D), lambda i, s: (0, 0))],
            out_specs=pl.BlockSpec((tm, D), lambda i, s: (i, 0)),
            scratch_shapes=[pltpu.VMEM((TOP_K, tm, D), F32), pltpu.SemaphoreType.DMA(())]),
        out_shape=jax.ShapeDtypeStruct((N_TOK, D), F32),
        compiler_params=_cparams(("arbitrary",), 40),
        name="moe_combine",
    )(slot, ys, top_w_t, x1, mod3, npost.reshape(1, D))


def _rope_tables():
    pos = jnp.arange(DEC_SEQ)
    row = (pos // GRID_W).astype(F32)
    col = (pos % GRID_W).astype(F32)
    n_freq = DK // 4
    freqs = THETA ** (-jnp.arange(n_freq, dtype=F32) / n_freq)
    ar = row[:, None] * freqs
    ac = col[:, None] * freqs
    cos_t = jnp.concatenate([jnp.cos(ar), jnp.cos(ar), jnp.cos(ac), jnp.cos(ac)], axis=1)
    sin_t = jnp.concatenate([-jnp.sin(ar), jnp.sin(ar), -jnp.sin(ac), jnp.sin(ac)], axis=1)
    return cos_t.astype(F32), sin_t.astype(F32)


def kernel(x_prompt, x_sample, state_ret, c, c_ctx, w_ada, b_ada, norm_mix_pre, norm_mix_post,
           norm_ffn_pre, norm_ffn_post, w_in, conv_w, conv_b, ret_decay_logit, ret_gn_w,
           w_br_conv, w_br_ret, w_out, router_w, router_b, moe_w_gate, moe_b_gate,
           moe_w_up, moe_b_up, moe_w_down, moe_b_down):
    layer = 0
    x = jnp.concatenate([x_prompt.reshape(N_CTX, D), x_sample.reshape(N_LAT, D)], axis=0)
    cvec = jnp.concatenate(
        [c_ctx[None, :], c, jnp.zeros((MOD_ROWS - 1 - DEC_BATCH, D), F32)], axis=0)

    mod = _modulation(cvec, w_ada[layer], b_ada[layer])
    mod3 = mod.reshape(MOD_ROWS, 1, N_MOD * D)

    h1 = _norm_mod(x, norm_mix_pre[layer], mod3)
    proj = _in_proj(h1, w_in[layer])

    a = _conv_branch(proj, conv_w[layer], conv_b[layer], w_br_conv[layer].astype(BF16))

    cos_t, sin_t = _rope_tables()
    dl = ret_decay_logit[layer].astype(F32)
    r_ctx, new_state = _retention(proj, dl, ret_gn_w[layer], None, None, None, latent=False)
    r_lat = _retention(proj, dl, ret_gn_w[layer], cos_t, sin_t, state_ret, latent=True)

    x1, h2, top_i, top_w = _mix_route(
        r_ctx, r_lat, w_br_ret[layer].astype(BF16), a, proj, w_out[layer].astype(BF16), x, mod3,
        norm_mix_post[layer], norm_ffn_pre[layer],
        router_w[layer].T.astype(BF16), router_b[layer])

    slot, tok_of_slot, tile_blk, tile_exp, n_tiles = _routing_tables(top_i)
    xs = _dispatch(h2, tok_of_slot, tile_blk, n_tiles)
    hact = _moe_up(xs, moe_w_gate[layer], moe_w_up[layer], moe_b_gate[layer], moe_b_up[layer],
                   tile_blk, tile_exp, n_tiles)
    ys = _moe_down(hact, moe_w_down[layer], moe_b_down[layer], tile_blk, tile_exp, n_tiles)
    x2 = _combine(ys, slot, top_w.T, x1, mod3, norm_ffn_post[layer])

    return (x2[:N_CTX].reshape(BATCH, SEQ, D), x2[N_CTX:].reshape(DEC_BATCH, DEC_SEQ, D),
            new_state)
```

```python
import functools

import jax
import jax.numpy as jnp
import numpy as np
from jax import lax
from jax.experimental import pallas as pl
from jax.experimental.pallas import tpu as pltpu

F32 = jnp.float32
BF16 = jnp.bfloat16
I32 = jnp.int32

D = 2048
BATCH = 16
SEQ = 256
DEC_BATCH = 2
DEC_SEQ = 1024
GRID_W = 64
CONV_W = 1024
HEADS = 8
DK = 128
DV = 256
CHUNK = 128
N_EXP = 32
TOP_K = 4
D_FF = 2048
LIMIT = 7.0
ALPHA = 1.702
THETA = 10000.0
EPS = 1e-6
N_MOD = 6
N_CTX = BATCH * SEQ
N_LAT = DEC_BATCH * DEC_SEQ
N_TOK = N_CTX + N_LAT
IN_COLS = 3 * CONV_W + 2 * HEADS * DK + 2 * HEADS * DV + 2 * D
COL_CX, COL_CB, COL_CC = 0, CONV_W, 2 * CONV_W
COL_Q = 3 * CONV_W
COL_K = COL_Q + HEADS * DK
COL_V = COL_K + HEADS * DK
COL_G = COL_V + HEADS * DV
COL_MGC = COL_G + HEADS * DV
COL_MGR = COL_MGC + D
MOD_SH1, MOD_SC1, MOD_G1, MOD_SH2, MOD_SC2, MOD_G2 = range(6)
MOD_ROWS = 8

MOE_TM = 256
N_PAIR = N_TOK * TOP_K
MOE_TILES = (N_PAIR + N_EXP * (MOE_TM - 1)) // MOE_TM + 1
MOE_ROWS = MOE_TILES * MOE_TM
MOE_SUB = 4
MOE_GROUPS = (MOE_TILES + (MOE_SUB - 1) * N_EXP) // MOE_SUB
MOE_CH = 256
MOE_NCH = D_FF // MOE_CH

MIB = 1024 * 1024


def _cparams(sem, vmem_mib):
    return pltpu.CompilerParams(dimension_semantics=sem, vmem_limit_bytes=vmem_mib * MIB)


def _mod_row(t0):
    return jnp.where(t0 < N_CTX, 0, 1 + (t0 - N_CTX) // DEC_SEQ)


def _mod_spec(tm, chunk):
    return pl.BlockSpec((None, 1, D), lambda i: (_mod_row(i * tm), 0, chunk))


def _rms(x, g):
    return (x * lax.rsqrt(jnp.mean(x * x, axis=-1, keepdims=True) + EPS)) * g


def _ada_kernel(c_ref, w_ref, b_ref, o_ref):
    c = c_ref[...]
    s = c * jax.nn.sigmoid(c)
    o_ref[...] = jnp.dot(s.astype(BF16), w_ref[...].astype(BF16),
                         preferred_element_type=F32) + b_ref[...]


def _modulation(cvec, w_ada, b_ada):
    tn = 1024
    return pl.pallas_call(
        _ada_kernel,
        grid=(N_MOD * D // tn,),
        in_specs=[pl.BlockSpec((MOD_ROWS, D), lambda j: (0, 0)),
                  pl.BlockSpec((D, tn), lambda j: (0, j)),
                  pl.BlockSpec((1, tn), lambda j: (0, j))],
        out_specs=pl.BlockSpec((MOD_ROWS, tn), lambda j: (0, j)),
        out_shape=jax.ShapeDtypeStruct((MOD_ROWS, N_MOD * D), F32),
        compiler_params=_cparams(("arbitrary",), 40),
        name="ada_modulation",
    )(cvec, w_ada, b_ada.reshape(1, -1))


NORM_TM = 512


def _ctx_lat_specs(tm, width):
    ctx_tiles = N_CTX // tm
    return (pl.BlockSpec((tm, width), lambda i, *_: (jnp.minimum(i, ctx_tiles - 1), 0)),
            pl.BlockSpec((tm, width), lambda i, *_: (jnp.maximum(i - ctx_tiles, 0), 0)))


def _norm_mod_kernel(xc_ref, xl_ref, g_ref, sh_ref, sc_ref, o_ref):
    x = jnp.where(pl.program_id(0) * NORM_TM < N_CTX, xc_ref[...], xl_ref[...])
    y = _rms(x, g_ref[...])
    o_ref[...] = (y * (1.0 + sc_ref[...]) + sh_ref[...]).astype(BF16)


def _norm_mod(x_ctx, x_lat, g, mod3):
    tm = NORM_TM
    return pl.pallas_call(
        _norm_mod_kernel,
        grid=(N_TOK // tm,),
        in_specs=[*_ctx_lat_specs(tm, D),
                  pl.BlockSpec((1, D), lambda i: (0, 0)),
                  _mod_spec(tm, MOD_SH1),
                  _mod_spec(tm, MOD_SC1)],
        out_specs=pl.BlockSpec((tm, D), lambda i: (i, 0)),
        out_shape=jax.ShapeDtypeStruct((N_TOK, D), BF16),
        compiler_params=_cparams(("arbitrary",), 40),
        name="norm_modulate",
    )(x_ctx, x_lat, g.reshape(1, D), mod3, mod3)


def _proj_kernel(h_ref, w_ref, o_ref, wb_ref):
    @pl.when(pl.program_id(1) == 0)
    def _():
        wb_ref[...] = w_ref[...].astype(BF16)

    o_ref[...] = jnp.dot(h_ref[...], wb_ref[...], preferred_element_type=F32).astype(BF16)


def _in_proj(h, w_in):
    tm, tn = 1024, 1024
    return pl.pallas_call(
        _proj_kernel,
        grid=(IN_COLS // tn, N_TOK // tm),
        in_specs=[pl.BlockSpec((tm, D), lambda n, m: (m, 0)),
                  pl.BlockSpec((D, tn), lambda n, m: (0, n))],
        out_specs=pl.BlockSpec((tm, tn), lambda n, m: (m, n)),
        out_shape=jax.ShapeDtypeStruct((N_TOK, IN_COLS), BF16),
        scratch_shapes=[pltpu.VMEM((D, tn), BF16)],
        compiler_params=_cparams(("arbitrary", "arbitrary"), 48),
        name="in_proj",
    )(h, w_in)


CONV_TM = 256


def _conv_kernel(cx_ref, cb_ref, cc_ref, mg0_ref, mg1_ref, cw_ref, cbias_ref, w_ref, o_ref):
    i = pl.program_id(0)
    u = cc_ref[...].astype(F32) * cx_ref[...].astype(F32)
    seg = jnp.where(i * CONV_TM < N_CTX, SEQ, GRID_W)
    pos = lax.broadcasted_iota(I32, (CONV_TM, 1), 0) & (seg - 1)
    prev = jnp.where(pos == 0, 0.0, pltpu.roll(u, 1, 0))
    nxt = jnp.where(pos == seg - 1, 0.0, pltpu.roll(u, CONV_TM - 1, 0))
    cw = cw_ref[...]
    uc = prev * cw[0:1, :] + u * cw[1:2, :] + nxt * cw[2:3, :] + cbias_ref[...]
    z = (cb_ref[...].astype(F32) * uc).astype(BF16)
    y = jnp.dot(z, w_ref[...], preferred_element_type=F32)
    mg = jnp.concatenate([mg0_ref[...], mg1_ref[...]], axis=1).astype(F32)
    o_ref[...] = (jax.nn.sigmoid(mg) * y).astype(BF16)


def _conv_branch(proj, conv_w, conv_b, w_br_conv_bf):
    tm = CONV_TM
    cw = 1024

    def col(c):
        return pl.BlockSpec((tm, cw), lambda i: (i, c))

    return pl.pallas_call(
        _conv_kernel,
        grid=(N_TOK // tm,),
        in_specs=[col(COL_CX // cw), col(COL_CB // cw), col(COL_CC // cw),
                  col(COL_MGC // cw), col(COL_MGC // cw + 1),
                  pl.BlockSpec((3, CONV_W), lambda i: (0, 0)),
                  pl.BlockSpec((1, CONV_W), lambda i: (0, 0)),
                  pl.BlockSpec((CONV_W, D), lambda i: (0, 0))],
        out_specs=pl.BlockSpec((tm, D), lambda i: (i, 0)),
        out_shape=jax.ShapeDtypeStruct((N_TOK, D), BF16),
        compiler_params=_cparams(("arbitrary",), 48),
        name="conv_branch",
    )(proj, proj, proj, proj, proj, conv_w, conv_b.reshape(1, CONV_W), w_br_conv_bf)


def _log_sigmoid(x):
    return -(jnp.maximum(-x, 0.0) + jnp.log1p(jnp.exp(-jnp.abs(x))))


def _rope(x, cos, sin_signed):
    lane = lax.broadcasted_iota(I32, x.shape, 1)
    swapped = jnp.where((lane & 63) < 32, pltpu.roll(x, 96, 1), pltpu.roll(x, 32, 1))
    return x * cos + swapped * sin_signed


def _retention_kernel(*refs, seq_len, latent):
    if latent:
        (dl_ref, q_ref, k_ref, v_ref, g_ref, gn_ref, cos_ref, sin_ref, s0_ref,
         r_ref, o_acc) = refs
    else:
        dl_ref, q_ref, k_ref, v_ref, g_ref, gn_ref, r_ref, sout_ref, o_acc = refs
    h = pl.program_id(1)
    c = CHUNK
    n_chunks = seq_len // c

    lgf = _log_sigmoid(jnp.full((c, c), dl_ref[0, h], F32))
    lgb = _log_sigmoid(jnp.full((c, c), dl_ref[1, h], F32))
    lgf_w = _log_sigmoid(jnp.full((c, DV), dl_ref[0, h], F32))
    lgb_w = _log_sigmoid(jnp.full((c, DV), dl_ref[1, h], F32))

    ii = lax.broadcasted_iota(I32, (c, c), 0).astype(F32)
    jj = lax.broadcasted_iota(I32, (c, c), 1).astype(F32)
    diff = ii - jj
    m_in = jnp.where(diff >= 0.0, jnp.exp(jnp.maximum(diff, 0.0) * lgf),
                     jnp.exp(jnp.maximum(-diff, 0.0) * lgb))
    io = lax.broadcasted_iota(I32, (c, DV), 0).astype(F32)
    dq_f = jnp.exp((io + 1.0) * lgf_w)
    dq_b = jnp.exp((c - io) * lgb_w)
    dk_f = jnp.exp((c - 1.0 - jj) * lgf)
    dk_b = jnp.exp(jj * lgb)
    dc_f = jnp.exp(c * lgf_w)
    dc_b = jnp.exp(c * lgb_w)

    q = q_ref[...].astype(F32)
    k = k_ref[...].astype(F32) * (DK ** -0.5)
    if latent:
        q = _rope(q, cos_ref[...], sin_ref[...])
        k = _rope(k, cos_ref[...], sin_ref[...])
        s_f = s0_ref[0]
        s_b = s0_ref[1]
    else:
        s_f = jnp.zeros((DK, DV), F32)
        s_b = jnp.zeros((DK, DV), F32)
    qb = q.astype(BF16)

    def chunk(x, n):
        return x[n * c:(n + 1) * c]

    for n in range(n_chunks):
        qn = chunk(qb, n)
        kt = chunk(k, n).T
        vn = v_ref[n * c:(n + 1) * c, :]
        s = jnp.dot(qn, kt.astype(BF16), preferred_element_type=F32)
        p = (s * m_in).astype(BF16)
        o = jnp.dot(p, vn, preferred_element_type=F32)
        o = o + jnp.dot(qn, s_f.astype(BF16), preferred_element_type=F32) * dq_f
        o_acc[n * c:(n + 1) * c, :] = o
        s_f = dc_f * s_f + jnp.dot((kt * dk_f).astype(BF16), vn, preferred_element_type=F32)

    for n in reversed(range(n_chunks)):
        qn = chunk(qb, n)
        kt = chunk(k, n).T
        vn = v_ref[n * c:(n + 1) * c, :]
        o_acc[n * c:(n + 1) * c, :] += (
            jnp.dot(qn, s_b.astype(BF16), preferred_element_type=F32) * dq_b)
        s_b = dc_b * s_b + jnp.dot((kt * dk_b).astype(BF16), vn, preferred_element_type=F32)

    if not latent:
        sout_ref[0] = s_f
        sout_ref[1] = s_b

    o = o_acc[...]
    mu = jnp.mean(o, axis=-1, keepdims=True)
    oc = o - mu
    var = jnp.mean(oc * oc, axis=-1, keepdims=True)
    on = oc * lax.rsqrt(var + EPS) * gn_ref[...]
    g = g_ref[...].astype(F32)
    r_ref[...] = ((g * jax.nn.sigmoid(g)) * on).astype(BF16)


def _retention(proj, decay_logit, gn_w, cos_t, sin_t, state0, *, latent):
    seq_len = DEC_SEQ if latent else SEQ
    nb = DEC_BATCH if latent else BATCH
    rb0 = (N_CTX // DEC_SEQ) if latent else 0

    def spec(width, col0):
        return pl.BlockSpec((seq_len, width), lambda b, h, dl: (rb0 + b, col0 // width + h))

    in_specs = [spec(DK, COL_Q), spec(DK, COL_K), spec(DV, COL_V), spec(DV, COL_G),
                pl.BlockSpec((1, DV), lambda b, h, dl: (0, h))]
    args = [proj, proj, proj, proj, gn_w.reshape(1, HEADS * DV)]
    state_spec = pl.BlockSpec((None, None, 2, None, DK, DV), lambda b, h, dl: (b, 0, 0, h, 0, 0))
    r_shape = jax.ShapeDtypeStruct((nb * seq_len, HEADS * DV), BF16)
    r_spec = pl.BlockSpec((seq_len, DV), lambda b, h, dl: (b, h))
    if latent:
        in_specs += [pl.BlockSpec((seq_len, DK), lambda b, h, dl: (0, 0)),
                     pl.BlockSpec((seq_len, DK), lambda b, h, dl: (0, 0)),
                     state_spec]
        args += [cos_t, sin_t, state0]
        out_specs, out_shape = r_spec, r_shape
    else:
        out_specs = (r_spec, state_spec)
        out_shape = (r_shape, jax.ShapeDtypeStruct((BATCH, 1, 2, HEADS, DK, DV), F32))
    return pl.pallas_call(
        functools.partial(_retention_kernel, seq_len=seq_len, latent=latent),
        grid_spec=pltpu.PrefetchScalarGridSpec(
            num_scalar_prefetch=1, grid=(nb, HEADS), in_specs=in_specs, out_specs=out_specs,
            scratch_shapes=[pltpu.VMEM((seq_len, DV), F32)]),
        out_shape=out_shape,
        compiler_params=_cparams(("arbitrary", "arbitrary"), 40),
        name="retention_latent" if latent else "retention_context",
    )(decay_logit, *args)


MIX_TM = 256


def _mix_kernel(rc_ref, rl_ref, wbr_ref, a_ref, mg0_ref, mg1_ref, wout_ref, xc_ref, xl_ref, g1_ref,
                sc2_ref, sh2_ref, npost_ref, npre_ref, rwt_ref, rb_ref,
                x1_ref, h2_ref, ti_ref, tw_ref):
    is_ctx = pl.program_id(0) * MIX_TM < N_CTX
    r = jnp.where(is_ctx, rc_ref[...], rl_ref[...])
    x = jnp.where(is_ctx, xc_ref[...], xl_ref[...])
    y = jnp.dot(r, wbr_ref[...], preferred_element_type=F32)
    mg = jnp.concatenate([mg0_ref[...], mg1_ref[...]], axis=1).astype(F32)
    mix = a_ref[...].astype(F32) + jax.nn.sigmoid(mg) * y
    out = jnp.dot(mix.astype(BF16), wout_ref[...], preferred_element_type=F32)
    x1 = x + g1_ref[...] * _rms(out, npost_ref[...])
    x1_ref[...] = x1
    h2 = _rms(x1, npre_ref[...]) * (1.0 + sc2_ref[...]) + sh2_ref[...]
    h2_ref[...] = h2
    logits = lax.dot_general(rwt_ref[...], h2.astype(BF16), (((1,), (1,)), ((), ())),
                             preferred_element_type=F32) + rb_ref[...]
    eidx = lax.broadcasted_iota(I32, logits.shape, 0).astype(F32)
    vals, idxs = [], []
    for _ in range(TOP_K):
        m = jnp.max(logits, axis=0, keepdims=True)
        am = jnp.min(jnp.where(logits == m, eidx, float(N_EXP)), axis=0, keepdims=True)
        vals.append(m)
        idxs.append(am)
        logits = jnp.where(eidx == am, -jnp.inf, logits)
    es = [jnp.exp(v - vals[0]) for v in vals]
    denom = es[0] + es[1] + es[2] + es[3]
    for kk in range(TOP_K):
        ti_ref[kk:kk + 1, :] = idxs[kk].astype(I32)
        tw_ref[kk:kk + 1, :] = es[kk] / denom


def _mix_route(r_ctx, r_lat, wbr_bf, a, proj, wout_bf, x_ctx, x_lat, mod3, npost, npre, rwt_bf, rb):
    tm = MIX_TM
    cw = 1024

    def full(shape):
        return pl.BlockSpec(shape, lambda i: (0,) * len(shape), pipeline_mode=pl.Buffered(1))

    def row(width):
        return pl.BlockSpec((tm, width), lambda i: (i, 0))

    return pl.pallas_call(
        _mix_kernel,
        grid=(N_TOK // tm,),
        in_specs=[*_ctx_lat_specs(tm, D),
                  full((D, D)), row(D),
                  pl.BlockSpec((tm, cw), lambda i: (i, COL_MGR // cw)),
                  pl.BlockSpec((tm, cw), lambda i: (i, COL_MGR // cw + 1)),
                  full((D, D)), *_ctx_lat_specs(tm, D),
                  _mod_spec(tm, MOD_G1), _mod_spec(tm, MOD_SC2), _mod_spec(tm, MOD_SH2),
                  full((1, D)), full((1, D)), full((N_EXP, D)), full((N_EXP, 1))],
        out_specs=(row(D), row(D),
                   pl.BlockSpec((TOP_K, tm), lambda i: (0, i)),
                   pl.BlockSpec((TOP_K, tm), lambda i: (0, i))),
        out_shape=(jax.ShapeDtypeStruct((N_TOK, D), F32),
                   jax.ShapeDtypeStruct((N_TOK, D), F32),
                   jax.ShapeDtypeStruct((TOP_K, N_TOK), I32),
                   jax.ShapeDtypeStruct((TOP_K, N_TOK), F32)),
        compiler_params=_cparams(("arbitrary",), 56),
        name="mix_out_route",
    )(r_ctx, r_lat, wbr_bf, a, proj, proj, wout_bf, x_ctx, x_lat, mod3, mod3, mod3,
      npost.reshape(1, D), npre.reshape(1, D), rwt_bf, rb.reshape(N_EXP, 1))


def _routing_tables(top_i):
    e = top_i.reshape(-1)
    onehot = (e[:, None] == jnp.arange(N_EXP, dtype=I32)[None, :]).astype(I32)
    csum = jnp.cumsum(onehot, axis=0)
    rank = jnp.take_along_axis(csum, e[:, None], axis=1)[:, 0] - 1
    counts = csum[-1]
    tiles_e = (counts + MOE_TM - 1) // MOE_TM
    tile_end = jnp.cumsum(tiles_e)
    tile_start = tile_end - tiles_e
    slot = tile_start[e] * MOE_TM + rank
    n_tiles = tile_end[-1]
    tok = jnp.tile(jnp.arange(N_TOK, dtype=I32), TOP_K)
    tok_of_slot = jnp.zeros((MOE_ROWS,), I32).at[slot].set(tok)

    groups_e = (tiles_e + MOE_SUB - 1) // MOE_SUB
    g_end = jnp.cumsum(groups_e)
    g_start = g_end - groups_e
    n_groups = g_end[-1]
    gid = jnp.arange(MOE_GROUPS, dtype=I32)
    used = gid < n_groups
    g_exp = jnp.minimum(jnp.sum((gid[:, None] >= g_end[None, :]).astype(I32), axis=1), N_EXP - 1)
    g_exp = jnp.where(used, g_exp, g_exp[jnp.maximum(n_groups - 1, 0)])
    g_in_e = gid - g_start[g_exp]
    first = tile_start[g_exp] + MOE_SUB * g_in_e
    n_valid = jnp.where(used, jnp.clip(tiles_e[g_exp] - MOE_SUB * g_in_e, 0, MOE_SUB), 0)
    sub = jnp.arange(MOE_SUB, dtype=I32)[None, :]
    x_tile = first[:, None] + jnp.minimum(sub, jnp.maximum(n_valid[:, None] - 1, 0))
    x_tile = jnp.where(used[:, None], x_tile, x_tile[jnp.maximum(n_groups - 1, 0)][None, :])
    tail = n_tiles + (gid[:, None] - n_groups) * MOE_SUB + sub
    o_tile = jnp.where(used[:, None],
                       jnp.where(sub < n_valid[:, None], first[:, None] + sub, -1),
                       jnp.where(tail < MOE_TILES, tail, -1))
    return (slot, tok_of_slot, n_tiles.reshape(1).astype(I32), x_tile.reshape(-1).astype(I32),
            o_tile.reshape(-1).astype(I32), n_valid.astype(I32), g_exp.astype(I32))


def _row_copy(src_hbm, row, dst_ref, r, sem):
    return pltpu.make_async_copy(src_hbm.at[pl.ds(row, 1), :], dst_ref.at[pl.ds(r, 1), :], sem)


def _dispatch_kernel(tos_ref, nt_ref, x_hbm, o_ref, buf, sem):
    j = pl.program_id(0)
    nt = nt_ref[0]

    @pl.when(j < nt)
    def _():
        b = j & 1
        base = j * MOE_TM

        def issue(r, carry):
            _row_copy(x_hbm, tos_ref[base + r], buf.at[b], r, sem.at[b]).start()
            return carry

        lax.fori_loop(0, MOE_TM, issue, 0, unroll=8)

    @pl.when(jnp.logical_and(j >= 1, j - 1 < nt))
    def _():
        b = (j - 1) & 1

        def wait(r, carry):
            _row_copy(x_hbm, 0, buf.at[b], r, sem.at[b]).wait()
            return carry

        lax.fori_loop(0, MOE_TM, wait, 0, unroll=8)
        o_ref[...] = buf[b].astype(BF16)

    @pl.when(jnp.logical_and(j >= 1, j - 1 >= nt))
    def _():
        o_ref[...] = jnp.zeros_like(o_ref)


def _dispatch(h2, tok_of_slot, n_tiles):
    return pl.pallas_call(
        _dispatch_kernel,
        grid_spec=pltpu.PrefetchScalarGridSpec(
            num_scalar_prefetch=2, grid=(MOE_TILES + 1,),
            in_specs=[pl.BlockSpec(memory_space=pl.ANY)],
            out_specs=pl.BlockSpec((MOE_TM, D), lambda j, tos, nt: (jnp.maximum(j - 1, 0), 0)),
            scratch_shapes=[pltpu.VMEM((2, MOE_TM, D), F32), pltpu.SemaphoreType.DMA((2,))]),
        out_shape=jax.ShapeDtypeStruct((MOE_ROWS, D), BF16),
        compiler_params=_cparams(("arbitrary",), 32),
        name="moe_dispatch",
    )(tok_of_slot, n_tiles, h2)


def _tile_store(acc_ref, b, i, y_hbm, tile, sem):
    return pltpu.make_async_copy(acc_ref.at[b, i], y_hbm.at[pl.ds(tile * MOE_TM, MOE_TM), :],
                                 sem.at[b])


def _moe_kernel(xt_ref, ot_ref, nv_ref, ge_ref, *refs):
    x_refs = refs[:MOE_SUB]
    wg_ref, wu_ref, wd_ref, bg_ref, bu_ref, bd_ref, y_hbm = refs[MOE_SUB:MOE_SUB + 7]
    wgb_ref, wub_ref, wdb_ref, acc_ref, sem = refs[MOE_SUB + 7:]
    g = pl.program_id(0)
    c = pl.program_id(1)
    nv = nv_ref[g]
    b = g & 1
    last_chunk = c == MOE_NCH - 1

    def wait_group(gw):
        for i in range(MOE_SUB):
            @pl.when(ot_ref[gw * MOE_SUB + i] >= 0)
            def _(i=i):
                _tile_store(acc_ref, gw & 1, i, y_hbm, 0, sem).wait()

    @pl.when(jnp.logical_and(c == 0, g >= 2))
    def _():
        wait_group(g - 2)

    @pl.when(nv > 0)
    def _():
        wgb_ref[...] = wg_ref[...].astype(BF16)
        wub_ref[...] = wu_ref[...].astype(BF16)
        wdb_ref[...] = wd_ref[...].astype(BF16)

    for i in range(MOE_SUB):
        @pl.when(i < nv)
        def _(i=i):
            x = x_refs[i][...]
            gate = jnp.dot(x, wgb_ref[...], preferred_element_type=F32) + bg_ref[...]
            up = jnp.dot(x, wub_ref[...], preferred_element_type=F32) + bu_ref[...]
            gate = jnp.minimum(gate, LIMIT)
            up = jnp.clip(up, -LIMIT, LIMIT)
            act = (gate * jax.nn.sigmoid(ALPHA * gate) * (up + 1.0)).astype(BF16)
            y = jnp.dot(act, wdb_ref[...], preferred_element_type=F32)

            @pl.when(c == 0)
            def _():
                acc_ref[b, i] = y + bd_ref[...]

            @pl.when(c > 0)
            def _():
                acc_ref[b, i] += y

        @pl.when(jnp.logical_and(jnp.logical_and(i >= nv, last_chunk), ot_ref[g * MOE_SUB + i] >= 0))
        def _(i=i):
            acc_ref[b, i] = jnp.zeros((MOE_TM, D), F32)

        @pl.when(jnp.logical_and(last_chunk, ot_ref[g * MOE_SUB + i] >= 0))
        def _(i=i):
            _tile_store(acc_ref, b, i, y_hbm, ot_ref[g * MOE_SUB + i], sem).start()

    @pl.when(jnp.logical_and(last_chunk, g == MOE_GROUPS - 1))
    def _():
        wait_group(g - 1)
        wait_group(g)


def _moe(xs, wg, wu, wd, bg, bu, bd, x_tile, o_tile, n_valid, g_exp):
    def chunk_of(g, c, nv):
        return jnp.where(nv[g] > 0, c, MOE_NCH - 1)

    def x_spec(i):
        return pl.BlockSpec((MOE_TM, D), lambda g, c, xt, ot, nv, ge: (xt[g * MOE_SUB + i], 0))

    w_in_spec = pl.BlockSpec((None, D, MOE_CH),
                             lambda g, c, xt, ot, nv, ge: (ge[g], 0, chunk_of(g, c, nv)))
    b_in_spec = pl.BlockSpec((None, 1, MOE_CH),
                             lambda g, c, xt, ot, nv, ge: (ge[g], 0, chunk_of(g, c, nv)))
    w_out_spec = pl.BlockSpec((None, MOE_CH, D),
                              lambda g, c, xt, ot, nv, ge: (ge[g], chunk_of(g, c, nv), 0))
    b_out_spec = pl.BlockSpec((None, 1, D), lambda g, c, xt, ot, nv, ge: (ge[g], 0, 0))
    return pl.pallas_call(
        _moe_kernel,
        grid_spec=pltpu.PrefetchScalarGridSpec(
            num_scalar_prefetch=4, grid=(MOE_GROUPS, MOE_NCH),
            in_specs=[*[x_spec(i) for i in range(MOE_SUB)],
                      w_in_spec, w_in_spec, w_out_spec, b_in_spec, b_in_spec, b_out_spec],
            out_specs=pl.BlockSpec(memory_space=pl.ANY),
            scratch_shapes=[pltpu.VMEM((D, MOE_CH), BF16), pltpu.VMEM((D, MOE_CH), BF16),
                            pltpu.VMEM((MOE_CH, D), BF16),
                            pltpu.VMEM((2, MOE_SUB, MOE_TM, D), F32),
                            pltpu.SemaphoreType.DMA((2,))]),
        out_shape=jax.ShapeDtypeStruct((MOE_ROWS, D), F32),
        compiler_params=_cparams(("arbitrary", "arbitrary"), 56),
        name="moe_experts",
    )(x_tile, o_tile, n_valid, g_exp, *([xs] * MOE_SUB), wg, wu, wd,
      bg.reshape(N_EXP, 1, D_FF), bu.reshape(N_EXP, 1, D_FF), bd.reshape(N_EXP, 1, D))


COMB_TM = 256


def _combine_kernel(slot_ref, ys_hbm, w_ref, x1_ref, g2_ref, npost_ref, oc_ref, ol_ref, buf, sem):
    i = pl.program_id(0)
    base = i * COMB_TM
    for kk in range(TOP_K):
        def issue(r, carry, kk=kk):
            _row_copy(ys_hbm, slot_ref[kk * N_TOK + base + r], buf.at[kk], r, sem).start()
            return carry

        lax.fori_loop(0, COMB_TM, issue, 0, unroll=8)
    for kk in range(TOP_K):
        def wait(r, carry, kk=kk):
            _row_copy(ys_hbm, 0, buf.at[kk], r, sem).wait()
            return carry

        lax.fori_loop(0, COMB_TM, wait, 0, unroll=8)
    w = w_ref[...]
    f = buf[0] * w[:, 0:1]
    for kk in range(1, TOP_K):
        f = f + buf[kk] * w[:, kk:kk + 1]
    out = x1_ref[...] + g2_ref[...] * _rms(f, npost_ref[...])

    @pl.when(base < N_CTX)
    def _():
        oc_ref[...] = out

    @pl.when(base >= N_CTX)
    def _():
        ol_ref[...] = out


def _combine(ys, slot, top_w_t, x1, mod3, npost):
    tm = COMB_TM
    return pl.pallas_call(
        _combine_kernel,
        grid_spec=pltpu.PrefetchScalarGridSpec(
            num_scalar_prefetch=1, grid=(N_TOK // tm,),
            in_specs=[pl.BlockSpec(memory_space=pl.ANY),
                      pl.BlockSpec((tm, TOP_K), lambda i, s: (i, 0)),
                      pl.BlockSpec((tm, D), lambda i, s: (i, 0)),
                      pl.BlockSpec((None, 1, D), lambda i, s: (_mod_row(i * tm), 0, MOD_G2)),
                      pl.BlockSpec((1, D), lambda i, s: (0, 0))],
            out_specs=_ctx_lat_specs(tm, D),
            scratch_shapes=[pltpu.VMEM((TOP_K, tm, D), F32), pltpu.SemaphoreType.DMA(())]),
        out_shape=(jax.ShapeDtypeStruct((N_CTX, D), F32), jax.ShapeDtypeStruct((N_LAT, D), F32)),
        compiler_params=_cparams(("arbitrary",), 40),
        name="moe_combine",
    )(slot, ys, top_w_t, x1, mod3, npost.reshape(1, D))


def _rope_tables():
    pos = jnp.arange(DEC_SEQ)
    row = (pos // GRID_W).astype(F32)
    col = (pos % GRID_W).astype(F32)
    n_freq = DK // 4
    freqs = THETA ** (-jnp.arange(n_freq, dtype=F32) / n_freq)
    ar = row[:, None] * freqs
    ac = col[:, None] * freqs
    cos_t = jnp.concatenate([jnp.cos(ar), jnp.cos(ar), jnp.cos(ac), jnp.cos(ac)], axis=1)
    sin_t = jnp.concatenate([-jnp.sin(ar), jnp.sin(ar), -jnp.sin(ac), jnp.sin(ac)], axis=1)
    return cos_t.astype(F32), sin_t.astype(F32)


def kernel(x_prompt, x_sample, state_ret, c, c_ctx, w_ada, b_ada, norm_mix_pre, norm_mix_post,
           norm_ffn_pre, norm_ffn_post, w_in, conv_w, conv_b, ret_decay_logit, ret_gn_w,
           w_br_conv, w_br_ret, w_out, router_w, router_b, moe_w_gate, moe_b_gate,
           moe_w_up, moe_b_up, moe_w_down, moe_b_down):
    layer = 0
    x_ctx = x_prompt.reshape(N_CTX, D)
    x_lat = x_sample.reshape(N_LAT, D)
    cvec = jnp.concatenate(
        [c_ctx[None, :], c, jnp.zeros((MOD_ROWS - 1 - DEC_BATCH, D), F32)], axis=0)

    mod = _modulation(cvec, w_ada[layer], b_ada[layer])
    mod3 = mod.reshape(MOD_ROWS, 1, N_MOD * D)

    h1 = _norm_mod(x_ctx, x_lat, norm_mix_pre[layer], mod3)
    proj = _in_proj(h1, w_in[layer])

    a = _conv_branch(proj, conv_w[layer], conv_b[layer], w_br_conv[layer].astype(BF16))

    cos_t, sin_t = _rope_tables()
    dl = ret_decay_logit[layer].astype(F32)
    r_ctx, new_state = _retention(proj, dl, ret_gn_w[layer], None, None, None, latent=False)
    r_lat = _retention(proj, dl, ret_gn_w[layer], cos_t, sin_t, state_ret, latent=True)

    x1, h2, top_i, top_w = _mix_route(
        r_ctx, r_lat, w_br_ret[layer].astype(BF16), a, proj, w_out[layer].astype(BF16),
        x_ctx, x_lat, mod3, norm_mix_post[layer], norm_ffn_pre[layer],
        router_w[layer].T.astype(BF16), router_b[layer])

    slot, tok_of_slot, n_tiles, x_tile, o_tile, n_valid, g_exp = _routing_tables(top_i)
    xs = _dispatch(h2, tok_of_slot, n_tiles)
    ys = _moe(xs, moe_w_gate[layer], moe_w_up[layer], moe_w_down[layer],
              moe_b_gate[layer], moe_b_up[layer], moe_b_down[layer],
              x_tile, o_tile, n_valid, g_exp)
    y_ctx, y_lat = _combine(ys, slot, top_w.T, x1, mod3, norm_ffn_post[layer])

    return (y_ctx.reshape(BATCH, SEQ, D), y_lat.reshape(DEC_BATCH, DEC_SEQ, D), new_state)
```

```python
import functools

import jax
import jax.numpy as jnp
import numpy as np
from jax import lax
from jax.experimental import pallas as pl
from jax.experimental.pallas import tpu as pltpu

F32 = jnp.float32
BF16 = jnp.bfloat16
I32 = jnp.int32

D = 2048
BATCH = 16
SEQ = 256
DEC_BATCH = 2
DEC_SEQ = 1024
GRID_W = 64
CONV_W = 1024
HEADS = 8
DK = 128
DV = 256
CHUNK = 128
N_EXP = 32
TOP_K = 4
D_FF = 2048
LIMIT = 7.0
ALPHA = 1.702
THETA = 10000.0
EPS = 1e-6
N_MOD = 6
N_CTX = BATCH * SEQ
N_LAT = DEC_BATCH * DEC_SEQ
N_TOK = N_CTX + N_LAT
IN_COLS = 3 * CONV_W + 2 * HEADS * DK + 2 * HEADS * DV + 2 * D
COL_CX, COL_CB, COL_CC = 0, CONV_W, 2 * CONV_W
COL_Q = 3 * CONV_W
COL_K = COL_Q + HEADS * DK
COL_V = COL_K + HEADS * DK
COL_G = COL_V + HEADS * DV
COL_MGC = COL_G + HEADS * DV
COL_MGR = COL_MGC + D
MOD_SH1, MOD_SC1, MOD_G1, MOD_SH2, MOD_SC2, MOD_G2 = range(6)
MOD_ROWS = 8

MOE_TM = 256
N_PAIR = N_TOK * TOP_K
MOE_TILES = (N_PAIR + N_EXP * (MOE_TM - 1)) // MOE_TM + 1
MOE_ROWS = MOE_TILES * MOE_TM
MOE_SUB = 4
MOE_GROUPS = (MOE_TILES + (MOE_SUB - 1) * N_EXP) // MOE_SUB
MOE_CH = 512
MOE_NCH = D_FF // MOE_CH

MIB = 1024 * 1024


def _cparams(sem, vmem_mib):
    return pltpu.CompilerParams(dimension_semantics=sem, vmem_limit_bytes=vmem_mib * MIB)


def _mod_row(t0):
    return jnp.where(t0 < N_CTX, 0, 1 + (t0 - N_CTX) // DEC_SEQ)


def _mod_spec(tm, chunk):
    return pl.BlockSpec((None, 1, D), lambda i: (_mod_row(i * tm), 0, chunk))


def _rms(x, g):
    return (x * lax.rsqrt(jnp.mean(x * x, axis=-1, keepdims=True) + EPS)) * g


def _ada_kernel(c_ref, w_ref, b_ref, o_ref):
    c = c_ref[...]
    s = c * jax.nn.sigmoid(c)
    o_ref[...] = jnp.dot(s.astype(BF16), w_ref[...].astype(BF16),
                         preferred_element_type=F32) + b_ref[...]


def _modulation(cvec, w_ada, b_ada):
    tn = 1024
    return pl.pallas_call(
        _ada_kernel,
        grid=(N_MOD * D // tn,),
        in_specs=[pl.BlockSpec((MOD_ROWS, D), lambda j: (0, 0)),
                  pl.BlockSpec((D, tn), lambda j: (0, j)),
                  pl.BlockSpec((1, tn), lambda j: (0, j))],
        out_specs=pl.BlockSpec((MOD_ROWS, tn), lambda j: (0, j)),
        out_shape=jax.ShapeDtypeStruct((MOD_ROWS, N_MOD * D), F32),
        compiler_params=_cparams(("arbitrary",), 40),
        name="ada_modulation",
    )(cvec, w_ada, b_ada.reshape(1, -1))


NORM_TM = 512


def _ctx_lat_specs(tm, width):
    ctx_tiles = N_CTX // tm
    return (pl.BlockSpec((tm, width), lambda i, *_: (jnp.minimum(i, ctx_tiles - 1), 0)),
            pl.BlockSpec((tm, width), lambda i, *_: (jnp.maximum(i - ctx_tiles, 0), 0)))


def _norm_mod_kernel(xc_ref, xl_ref, g_ref, sh_ref, sc_ref, o_ref):
    x = jnp.where(pl.program_id(0) * NORM_TM < N_CTX, xc_ref[...], xl_ref[...])
    y = _rms(x, g_ref[...])
    o_ref[...] = (y * (1.0 + sc_ref[...]) + sh_ref[...]).astype(BF16)


def _norm_mod(x_ctx, x_lat, g, mod3):
    tm = NORM_TM
    return pl.pallas_call(
        _norm_mod_kernel,
        grid=(N_TOK // tm,),
        in_specs=[*_ctx_lat_specs(tm, D),
                  pl.BlockSpec((1, D), lambda i: (0, 0)),
                  _mod_spec(tm, MOD_SH1),
                  _mod_spec(tm, MOD_SC1)],
        out_specs=pl.BlockSpec((tm, D), lambda i: (i, 0)),
        out_shape=jax.ShapeDtypeStruct((N_TOK, D), BF16),
        compiler_params=_cparams(("arbitrary",), 40),
        name="norm_modulate",
    )(x_ctx, x_lat, g.reshape(1, D), mod3, mod3)


def _proj_kernel(h_ref, w_ref, o_ref, wb_ref):
    @pl.when(pl.program_id(1) == 0)
    def _():
        wb_ref[...] = w_ref[...].astype(BF16)

    o_ref[...] = jnp.dot(h_ref[...], wb_ref[...], preferred_element_type=F32).astype(BF16)


def _in_proj(h, w_in):
    tm, tn = 1024, 1024
    return pl.pallas_call(
        _proj_kernel,
        grid=(IN_COLS // tn, N_TOK // tm),
        in_specs=[pl.BlockSpec((tm, D), lambda n, m: (m, 0)),
                  pl.BlockSpec((D, tn), lambda n, m: (0, n))],
        out_specs=pl.BlockSpec((tm, tn), lambda n, m: (m, n)),
        out_shape=jax.ShapeDtypeStruct((N_TOK, IN_COLS), BF16),
        scratch_shapes=[pltpu.VMEM((D, tn), BF16)],
        compiler_params=_cparams(("arbitrary", "arbitrary"), 48),
        name="in_proj",
    )(h, w_in)


CONV_TM = 256


def _conv_kernel(cx_ref, cb_ref, cc_ref, mg0_ref, mg1_ref, cw_ref, cbias_ref, w_ref, o_ref):
    i = pl.program_id(0)
    u = cc_ref[...].astype(F32) * cx_ref[...].astype(F32)
    seg = jnp.where(i * CONV_TM < N_CTX, SEQ, GRID_W)
    pos = lax.broadcasted_iota(I32, (CONV_TM, 1), 0) & (seg - 1)
    prev = jnp.where(pos == 0, 0.0, pltpu.roll(u, 1, 0))
    nxt = jnp.where(pos == seg - 1, 0.0, pltpu.roll(u, CONV_TM - 1, 0))
    cw = cw_ref[...]
    uc = prev * cw[0:1, :] + u * cw[1:2, :] + nxt * cw[2:3, :] + cbias_ref[...]
    z = (cb_ref[...].astype(F32) * uc).astype(BF16)
    y = jnp.dot(z, w_ref[...], preferred_element_type=F32)
    mg = jnp.concatenate([mg0_ref[...], mg1_ref[...]], axis=1).astype(F32)
    o_ref[...] = (jax.nn.sigmoid(mg) * y).astype(BF16)


def _conv_branch(proj, conv_w, conv_b, w_br_conv_bf):
    tm = CONV_TM
    cw = 1024

    def col(c):
        return pl.BlockSpec((tm, cw), lambda i: (i, c))

    return pl.pallas_call(
        _conv_kernel,
        grid=(N_TOK // tm,),
        in_specs=[col(COL_CX // cw), col(COL_CB // cw), col(COL_CC // cw),
                  col(COL_MGC // cw), col(COL_MGC // cw + 1),
                  pl.BlockSpec((3, CONV_W), lambda i: (0, 0)),
                  pl.BlockSpec((1, CONV_W), lambda i: (0, 0)),
                  pl.BlockSpec((CONV_W, D), lambda i: (0, 0))],
        out_specs=pl.BlockSpec((tm, D), lambda i: (i, 0)),
        out_shape=jax.ShapeDtypeStruct((N_TOK, D), BF16),
        compiler_params=_cparams(("arbitrary",), 48),
        name="conv_branch",
    )(proj, proj, proj, proj, proj, conv_w, conv_b.reshape(1, CONV_W), w_br_conv_bf)


def _log_sigmoid(x):
    return -(jnp.maximum(-x, 0.0) + jnp.log1p(jnp.exp(-jnp.abs(x))))


def _rope(x, cos, sin_signed):
    lane = lax.broadcasted_iota(I32, x.shape, 1)
    swapped = jnp.where((lane & 63) < 32, pltpu.roll(x, 96, 1), pltpu.roll(x, 32, 1))
    return x * cos + swapped * sin_signed


def _retention_kernel(*refs, seq_len, latent):
    if latent:
        (dl_ref, q_ref, k_ref, v_ref, g_ref, gn_ref, cos_ref, sin_ref, s0_ref,
         r_ref, o_acc) = refs
    else:
        dl_ref, q_ref, k_ref, v_ref, g_ref, gn_ref, r_ref, sout_ref, o_acc = refs
    h = pl.program_id(1)
    c = CHUNK
    n_chunks = seq_len // c

    lgf = _log_sigmoid(jnp.full((c, c), dl_ref[0, h], F32))
    lgb = _log_sigmoid(jnp.full((c, c), dl_ref[1, h], F32))
    lgf_w = _log_sigmoid(jnp.full((c, DV), dl_ref[0, h], F32))
    lgb_w = _log_sigmoid(jnp.full((c, DV), dl_ref[1, h], F32))

    ii = lax.broadcasted_iota(I32, (c, c), 0).astype(F32)
    jj = lax.broadcasted_iota(I32, (c, c), 1).astype(F32)
    diff = ii - jj
    m_in = jnp.where(diff >= 0.0, jnp.exp(jnp.maximum(diff, 0.0) * lgf),
                     jnp.exp(jnp.maximum(-diff, 0.0) * lgb))
    io = lax.broadcasted_iota(I32, (c, DV), 0).astype(F32)
    dq_f = jnp.exp((io + 1.0) * lgf_w)
    dq_b = jnp.exp((c - io) * lgb_w)
    dk_f = jnp.exp((c - 1.0 - jj) * lgf)
    dk_b = jnp.exp(jj * lgb)
    dc_f = jnp.exp(c * lgf_w)
    dc_b = jnp.exp(c * lgb_w)

    q = q_ref[...].astype(F32)
    k = k_ref[...].astype(F32) * (DK ** -0.5)
    if latent:
        q = _rope(q, cos_ref[...], sin_ref[...])
        k = _rope(k, cos_ref[...], sin_ref[...])
        s_f = s0_ref[0]
        s_b = s0_ref[1]
    else:
        s_f = jnp.zeros((DK, DV), F32)
        s_b = jnp.zeros((DK, DV), F32)
    qb = q.astype(BF16)

    def chunk(x, n):
        return x[n * c:(n + 1) * c]

    for n in range(n_chunks):
        qn = chunk(qb, n)
        kt = chunk(k, n).T
        vn = v_ref[n * c:(n + 1) * c, :]
        s = jnp.dot(qn, kt.astype(BF16), preferred_element_type=F32)
        p = (s * m_in).astype(BF16)
        o = jnp.dot(p, vn, preferred_element_type=F32)
        o = o + jnp.dot(qn, s_f.astype(BF16), preferred_element_type=F32) * dq_f
        o_acc[n * c:(n + 1) * c, :] = o
        s_f = dc_f * s_f + jnp.dot((kt * dk_f).astype(BF16), vn, preferred_element_type=F32)

    for n in reversed(range(n_chunks)):
        qn = chunk(qb, n)
        kt = chunk(k, n).T
        vn = v_ref[n * c:(n + 1) * c, :]
        o_acc[n * c:(n + 1) * c, :] += (
            jnp.dot(qn, s_b.astype(BF16), preferred_element_type=F32) * dq_b)
        s_b = dc_b * s_b + jnp.dot((kt * dk_b).astype(BF16), vn, preferred_element_type=F32)

    if not latent:
        sout_ref[0] = s_f
        sout_ref[1] = s_b

    o = o_acc[...]
    mu = jnp.mean(o, axis=-1, keepdims=True)
    oc = o - mu
    var = jnp.mean(oc * oc, axis=-1, keepdims=True)
    on = oc * lax.rsqrt(var + EPS) * gn_ref[...]
    g = g_ref[...].astype(F32)
    r_ref[...] = ((g * jax.nn.sigmoid(g)) * on).astype(BF16)


def _retention(proj, decay_logit, gn_w, cos_t, sin_t, state0, *, latent):
    seq_len = DEC_SEQ if latent else SEQ
    nb = DEC_BATCH if latent else BATCH
    rb0 = (N_CTX // DEC_SEQ) if latent else 0

    def spec(width, col0):
        return pl.BlockSpec((seq_len, width), lambda b, h, dl: (rb0 + b, col0 // width + h))

    in_specs = [spec(DK, COL_Q), spec(DK, COL_K), spec(DV, COL_V), spec(DV, COL_G),
                pl.BlockSpec((1, DV), lambda b, h, dl: (0, h))]
    args = [proj, proj, proj, proj, gn_w.reshape(1, HEADS * DV)]
    state_spec = pl.BlockSpec((None, None, 2, None, DK, DV), lambda b, h, dl: (b, 0, 0, h, 0, 0))
    r_shape = jax.ShapeDtypeStruct((nb * seq_len, HEADS * DV), BF16)
    r_spec = pl.BlockSpec((seq_len, DV), lambda b, h, dl: (b, h))
    if latent:
        in_specs += [pl.BlockSpec((seq_len, DK), lambda b, h, dl: (0, 0)),
                     pl.BlockSpec((seq_len, DK), lambda b, h, dl: (0, 0)),
                     state_spec]
        args += [cos_t, sin_t, state0]
        out_specs, out_shape = r_spec, r_shape
    else:
        out_specs = (r_spec, state_spec)
        out_shape = (r_shape, jax.ShapeDtypeStruct((BATCH, 1, 2, HEADS, DK, DV), F32))
    return pl.pallas_call(
        functools.partial(_retention_kernel, seq_len=seq_len, latent=latent),
        grid_spec=pltpu.PrefetchScalarGridSpec(
            num_scalar_prefetch=1, grid=(nb, HEADS), in_specs=in_specs, out_specs=out_specs,
            scratch_shapes=[pltpu.VMEM((seq_len, DV), F32)]),
        out_shape=out_shape,
        compiler_params=_cparams(("arbitrary", "arbitrary"), 40),
        name="retention_latent" if latent else "retention_context",
    )(decay_logit, *args)


MIX_TM = 256


def _mix_kernel(rc_ref, rl_ref, wbr_ref, a_ref, mg0_ref, mg1_ref, wout_ref, xc_ref, xl_ref, g1_ref,
                sc2_ref, sh2_ref, npost_ref, npre_ref, rwt_ref, rb_ref,
                x1_ref, h2_ref, ti_ref, tw_ref):
    is_ctx = pl.program_id(0) * MIX_TM < N_CTX
    r = jnp.where(is_ctx, rc_ref[...], rl_ref[...])
    x = jnp.where(is_ctx, xc_ref[...], xl_ref[...])
    y = jnp.dot(r, wbr_ref[...], preferred_element_type=F32)
    mg = jnp.concatenate([mg0_ref[...], mg1_ref[...]], axis=1).astype(F32)
    mix = a_ref[...].astype(F32) + jax.nn.sigmoid(mg) * y
    out = jnp.dot(mix.astype(BF16), wout_ref[...], preferred_element_type=F32)
    x1 = x + g1_ref[...] * _rms(out, npost_ref[...])
    x1_ref[...] = x1
    h2 = _rms(x1, npre_ref[...]) * (1.0 + sc2_ref[...]) + sh2_ref[...]
    h2_ref[...] = h2
    logits = lax.dot_general(rwt_ref[...], h2.astype(BF16), (((1,), (1,)), ((), ())),
                             preferred_element_type=F32) + rb_ref[...]
    eidx = lax.broadcasted_iota(I32, logits.shape, 0).astype(F32)
    vals, idxs = [], []
    for _ in range(TOP_K):
        m = jnp.max(logits, axis=0, keepdims=True)
        am = jnp.min(jnp.where(logits == m, eidx, float(N_EXP)), axis=0, keepdims=True)
        vals.append(m)
        idxs.append(am)
        logits = jnp.where(eidx == am, -jnp.inf, logits)
    es = [jnp.exp(v - vals[0]) for v in vals]
    denom = es[0] + es[1] + es[2] + es[3]
    for kk in range(TOP_K):
        ti_ref[kk:kk + 1, :] = idxs[kk].astype(I32)
        tw_ref[kk:kk + 1, :] = es[kk] / denom


def _mix_route(r_ctx, r_lat, wbr_bf, a, proj, wout_bf, x_ctx, x_lat, mod3, npost, npre, rwt_bf, rb):
    tm = MIX_TM
    cw = 1024

    def full(shape):
        return pl.BlockSpec(shape, lambda i: (0,) * len(shape), pipeline_mode=pl.Buffered(1))

    def row(width):
        return pl.BlockSpec((tm, width), lambda i: (i, 0))

    return pl.pallas_call(
        _mix_kernel,
        grid=(N_TOK // tm,),
        in_specs=[*_ctx_lat_specs(tm, D),
                  full((D, D)), row(D),
                  pl.BlockSpec((tm, cw), lambda i: (i, COL_MGR // cw)),
                  pl.BlockSpec((tm, cw), lambda i: (i, COL_MGR // cw + 1)),
                  full((D, D)), *_ctx_lat_specs(tm, D),
                  _mod_spec(tm, MOD_G1), _mod_spec(tm, MOD_SC2), _mod_spec(tm, MOD_SH2),
                  full((1, D)), full((1, D)), full((N_EXP, D)), full((N_EXP, 1))],
        out_specs=(row(D), row(D),
                   pl.BlockSpec((TOP_K, tm), lambda i: (0, i)),
                   pl.BlockSpec((TOP_K, tm), lambda i: (0, i))),
        out_shape=(jax.ShapeDtypeStruct((N_TOK, D), F32),
                   jax.ShapeDtypeStruct((N_TOK, D), F32),
                   jax.ShapeDtypeStruct((TOP_K, N_TOK), I32),
                   jax.ShapeDtypeStruct((TOP_K, N_TOK), F32)),
        compiler_params=_cparams(("arbitrary",), 56),
        name="mix_out_route",
    )(r_ctx, r_lat, wbr_bf, a, proj, proj, wout_bf, x_ctx, x_lat, mod3, mod3, mod3,
      npost.reshape(1, D), npre.reshape(1, D), rwt_bf, rb.reshape(N_EXP, 1))


def _routing_tables(top_i):
    e = top_i.reshape(-1)
    onehot = (e[:, None] == jnp.arange(N_EXP, dtype=I32)[None, :]).astype(I32)
    csum = jnp.cumsum(onehot, axis=0)
    rank = jnp.take_along_axis(csum, e[:, None], axis=1)[:, 0] - 1
    counts = csum[-1]
    tiles_e = (counts + MOE_TM - 1) // MOE_TM
    tile_end = jnp.cumsum(tiles_e)
    tile_start = tile_end - tiles_e
    slot = tile_start[e] * MOE_TM + rank
    n_tiles = tile_end[-1]
    tok = jnp.tile(jnp.arange(N_TOK, dtype=I32), TOP_K)
    tok_of_slot = jnp.zeros((MOE_ROWS,), I32).at[slot].set(tok)

    groups_e = (tiles_e + MOE_SUB - 1) // MOE_SUB
    g_end = jnp.cumsum(groups_e)
    g_start = g_end - groups_e
    n_groups = g_end[-1]
    gid = jnp.arange(MOE_GROUPS, dtype=I32)
    used = gid < n_groups
    g_exp = jnp.minimum(jnp.sum((gid[:, None] >= g_end[None, :]).astype(I32), axis=1), N_EXP - 1)
    g_exp = jnp.where(used, g_exp, g_exp[jnp.maximum(n_groups - 1, 0)])
    g_in_e = gid - g_start[g_exp]
    first = tile_start[g_exp] + MOE_SUB * g_in_e
    n_valid = jnp.where(used, jnp.clip(tiles_e[g_exp] - MOE_SUB * g_in_e, 0, MOE_SUB), 0)
    sub = jnp.arange(MOE_SUB, dtype=I32)[None, :]
    x_tile = first[:, None] + jnp.minimum(sub, jnp.maximum(n_valid[:, None] - 1, 0))
    x_tile = jnp.where(used[:, None], x_tile, x_tile[jnp.maximum(n_groups - 1, 0)][None, :])
    tail = n_tiles + (gid[:, None] - n_groups) * MOE_SUB + sub
    o_tile = jnp.where(used[:, None],
                       jnp.where(sub < n_valid[:, None], first[:, None] + sub, -1),
                       jnp.where(tail < MOE_TILES, tail, -1))
    return (slot, tok_of_slot, n_tiles.reshape(1).astype(I32), x_tile.reshape(-1).astype(I32),
            o_tile.reshape(-1).astype(I32), n_valid.astype(I32), g_exp.astype(I32))


def _row_copy(src_hbm, row, dst_ref, r, sem):
    return pltpu.make_async_copy(src_hbm.at[pl.ds(row, 1), :], dst_ref.at[pl.ds(r, 1), :], sem)


def _dispatch_kernel(nt_ref, tos_ref, x_hbm, o_ref, buf, sem):
    j = pl.program_id(0)
    nt = nt_ref[0]

    @pl.when(j < nt)
    def _():
        b = j & 1

        def issue(r2, carry):
            for p in range(2):
                r = 2 * r2 + p
                _row_copy(x_hbm, tos_ref[0, r], buf.at[b], r, sem.at[b]).start(priority=p)
            return carry

        lax.fori_loop(0, MOE_TM // 2, issue, 0, unroll=4)

    @pl.when(jnp.logical_and(j >= 1, j - 1 < nt))
    def _():
        b = (j - 1) & 1

        def wait(r, carry):
            _row_copy(x_hbm, 0, buf.at[b], r, sem.at[b]).wait()
            return carry

        lax.fori_loop(0, MOE_TM, wait, 0, unroll=8)
        o_ref[...] = buf[b].astype(BF16)

    @pl.when(jnp.logical_and(j >= 1, j - 1 >= nt))
    def _():
        o_ref[...] = jnp.zeros_like(o_ref)


def _dispatch(h2, tok_of_slot, n_tiles):
    return pl.pallas_call(
        _dispatch_kernel,
        grid_spec=pltpu.PrefetchScalarGridSpec(
            num_scalar_prefetch=1, grid=(MOE_TILES + 1,),
            in_specs=[pl.BlockSpec((None, 1, MOE_TM),
                                   lambda j, nt: (jnp.minimum(j, MOE_TILES - 1), 0, 0),
                                   memory_space=pltpu.SMEM),
                      pl.BlockSpec(memory_space=pl.ANY)],
            out_specs=pl.BlockSpec((MOE_TM, D), lambda j, nt: (jnp.maximum(j - 1, 0), 0)),
            scratch_shapes=[pltpu.VMEM((2, MOE_TM, D), F32), pltpu.SemaphoreType.DMA((2,))]),
        out_shape=jax.ShapeDtypeStruct((MOE_ROWS, D), BF16),
        compiler_params=_cparams(("arbitrary",), 32),
        name="moe_dispatch",
    )(n_tiles, tok_of_slot.reshape(MOE_TILES, 1, MOE_TM), h2)


def _tile_store(acc_ref, b, i, y_hbm, tile, sem):
    return pltpu.make_async_copy(acc_ref.at[b, i], y_hbm.at[pl.ds(tile * MOE_TM, MOE_TM), :],
                                 sem.at[b])


def _moe_kernel(xt_ref, ot_ref, nv_ref, ge_ref, *refs):
    x_refs = refs[:MOE_SUB]
    wg_ref, wu_ref, wd_ref, bg_ref, bu_ref, bd_ref, y_hbm = refs[MOE_SUB:MOE_SUB + 7]
    wgb_ref, wub_ref, wdb_ref, acc_ref, sem = refs[MOE_SUB + 7:]
    g = pl.program_id(0)
    c = pl.program_id(1)
    nv = nv_ref[g]
    b = g & 1
    last_chunk = c == MOE_NCH - 1

    def wait_group(gw):
        for i in range(MOE_SUB):
            @pl.when(ot_ref[gw * MOE_SUB + i] >= 0)
            def _(i=i):
                _tile_store(acc_ref, gw & 1, i, y_hbm, 0, sem).wait()

    @pl.when(jnp.logical_and(c == 0, g >= 2))
    def _():
        wait_group(g - 2)

    @pl.when(jnp.logical_and(c == 0, nv > 0))
    def _():
        for i in range(MOE_SUB):
            acc_ref[b, i] = jnp.broadcast_to(bd_ref[...], (MOE_TM, D))

    def group_step(m):
        wgb_ref[...] = wg_ref[...].astype(BF16)
        wub_ref[...] = wu_ref[...].astype(BF16)
        wdb_ref[...] = wd_ref[...].astype(BF16)
        for i in range(m):
            x = x_refs[i][...]
            gate = jnp.dot(x, wgb_ref[...], preferred_element_type=F32) + bg_ref[...]
            up = jnp.dot(x, wub_ref[...], preferred_element_type=F32) + bu_ref[...]
            gate = jnp.minimum(gate, LIMIT)
            up = jnp.clip(up, -LIMIT, LIMIT)
            act = (gate * jax.nn.sigmoid(ALPHA * gate) * (up + 1.0)).astype(BF16)
            acc_ref[b, i] += jnp.dot(act, wdb_ref[...], preferred_element_type=F32)

    for m in range(1, MOE_SUB + 1):
        pl.when(nv == m)(functools.partial(group_step, m))

    for i in range(MOE_SUB):
        @pl.when(jnp.logical_and(jnp.logical_and(i >= nv, last_chunk), ot_ref[g * MOE_SUB + i] >= 0))
        def _(i=i):
            acc_ref[b, i] = jnp.zeros((MOE_TM, D), F32)

        @pl.when(jnp.logical_and(last_chunk, ot_ref[g * MOE_SUB + i] >= 0))
        def _(i=i):
            _tile_store(acc_ref, b, i, y_hbm, ot_ref[g * MOE_SUB + i], sem).start()

    @pl.when(jnp.logical_and(last_chunk, g == MOE_GROUPS - 1))
    def _():
        wait_group(g - 1)
        wait_group(g)


def _moe(xs, wg, wu, wd, bg, bu, bd, x_tile, o_tile, n_valid, g_exp):
    def chunk_of(g, c, nv):
        return jnp.where(nv[g] > 0, c, MOE_NCH - 1)

    def x_spec(i):
        return pl.BlockSpec((MOE_TM, D), lambda g, c, xt, ot, nv, ge: (xt[g * MOE_SUB + i], 0))

    w_in_spec = pl.BlockSpec((None, D, MOE_CH),
                             lambda g, c, xt, ot, nv, ge: (ge[g], 0, chunk_of(g, c, nv)))
    b_in_spec = pl.BlockSpec((None, 1, MOE_CH),
                             lambda g, c, xt, ot, nv, ge: (ge[g], 0, chunk_of(g, c, nv)))
    w_out_spec = pl.BlockSpec((None, MOE_CH, D),
                              lambda g, c, xt, ot, nv, ge: (ge[g], chunk_of(g, c, nv), 0))
    b_out_spec = pl.BlockSpec((None, 1, D), lambda g, c, xt, ot, nv, ge: (ge[g], 0, 0))
    return pl.pallas_call(
        _moe_kernel,
        grid_spec=pltpu.PrefetchScalarGridSpec(
            num_scalar_prefetch=4, grid=(MOE_GROUPS, MOE_NCH),
            in_specs=[*[x_spec(i) for i in range(MOE_SUB)],
                      w_in_spec, w_in_spec, w_out_spec, b_in_spec, b_in_spec, b_out_spec],
            out_specs=pl.BlockSpec(memory_space=pl.ANY),
            scratch_shapes=[pltpu.VMEM((D, MOE_CH), BF16), pltpu.VMEM((D, MOE_CH), BF16),
                            pltpu.VMEM((MOE_CH, D), BF16),
                            pltpu.VMEM((2, MOE_SUB, MOE_TM, D), F32),
                            pltpu.SemaphoreType.DMA((2,))]),
        out_shape=jax.ShapeDtypeStruct((MOE_ROWS, D), F32),
        compiler_params=_cparams(("arbitrary", "arbitrary"), 62),
        name="moe_experts",
    )(x_tile, o_tile, n_valid, g_exp, *([xs] * MOE_SUB), wg, wu, wd,
      bg.reshape(N_EXP, 1, D_FF), bu.reshape(N_EXP, 1, D_FF), bd.reshape(N_EXP, 1, D))


COMB_TM = 256


def _combine_kernel(slot_ref, ys_hbm, w_ref, x1_ref, g2_ref, npost_ref, oc_ref, ol_ref, buf, sem):
    i = pl.program_id(0)
    base = i * COMB_TM
    for kk in range(TOP_K):
        def issue(r2, carry, kk=kk):
            for p in range(2):
                r = 2 * r2 + p
                _row_copy(ys_hbm, slot_ref[kk, r], buf.at[kk], r, sem).start(priority=p)
            return carry

        lax.fori_loop(0, COMB_TM // 2, issue, 0, unroll=4)
    for kk in range(TOP_K):
        def wait(r, carry, kk=kk):
            _row_copy(ys_hbm, 0, buf.at[kk], r, sem).wait()
            return carry

        lax.fori_loop(0, COMB_TM, wait, 0, unroll=8)
    w = w_ref[...]
    f = buf[0] * w[:, 0:1]
    for kk in range(1, TOP_K):
        f = f + buf[kk] * w[:, kk:kk + 1]
    out = x1_ref[...] + g2_ref[...] * _rms(f, npost_ref[...])

    @pl.when(base < N_CTX)
    def _():
        oc_ref[...] = out

    @pl.when(base >= N_CTX)
    def _():
        ol_ref[...] = out


def _combine(ys, slot, top_w_t, x1, mod3, npost):
    tm = COMB_TM
    return pl.pallas_call(
        _combine_kernel,
        grid=(N_TOK // tm,),
        in_specs=[pl.BlockSpec((TOP_K, tm), lambda i: (0, i), memory_space=pltpu.SMEM),
                  pl.BlockSpec(memory_space=pl.ANY),
                  pl.BlockSpec((tm, TOP_K), lambda i: (i, 0)),
                  pl.BlockSpec((tm, D), lambda i: (i, 0)),
                  _mod_spec(tm, MOD_G2),
                  pl.BlockSpec((1, D), lambda i: (0, 0))],
        out_specs=_ctx_lat_specs(tm, D),
        scratch_shapes=[pltpu.VMEM((TOP_K, tm, D), F32), pltpu.SemaphoreType.DMA(())],
        out_shape=(jax.ShapeDtypeStruct((N_CTX, D), F32), jax.ShapeDtypeStruct((N_LAT, D), F32)),
        compiler_params=_cparams(("arbitrary",), 40),
        name="moe_combine",
    )(slot.reshape(TOP_K, N_TOK), ys, top_w_t, x1, mod3, npost.reshape(1, D))


def _rope_tables():
    pos = jnp.arange(DEC_SEQ)
    row = (pos // GRID_W).astype(F32)
    col = (pos % GRID_W).astype(F32)
    n_freq = DK // 4
    freqs = THETA ** (-jnp.arange(n_freq, dtype=F32) / n_freq)
    ar = row[:, None] * freqs
    ac = col[:, None] * freqs
    cos_t = jnp.concatenate([jnp.cos(ar), jnp.cos(ar), jnp.cos(ac), jnp.cos(ac)], axis=1)
    sin_t = jnp.concatenate([-jnp.sin(ar), jnp.sin(ar), -jnp.sin(ac), jnp.sin(ac)], axis=1)
    return cos_t.astype(F32), sin_t.astype(F32)


def kernel(x_prompt, x_sample, state_ret, c, c_ctx, w_ada, b_ada, norm_mix_pre, norm_mix_post,
           norm_ffn_pre, norm_ffn_post, w_in, conv_w, conv_b, ret_decay_logit, ret_gn_w,
           w_br_conv, w_br_ret, w_out, router_w, router_b, moe_w_gate, moe_b_gate,
           moe_w_up, moe_b_up, moe_w_down, moe_b_down):
    layer = 0
    x_ctx = x_prompt.reshape(N_CTX, D)
    x_lat = x_sample.reshape(N_LAT, D)
    cvec = jnp.concatenate(
        [c_ctx[None, :], c, jnp.zeros((MOD_ROWS - 1 - DEC_BATCH, D), F32)], axis=0)

    mod = _modulation(cvec, w_ada[layer], b_ada[layer])
    mod3 = mod.reshape(MOD_ROWS, 1, N_MOD * D)

    h1 = _norm_mod(x_ctx, x_lat, norm_mix_pre[layer], mod3)
    proj = _in_proj(h1, w_in[layer])

    a = _conv_branch(proj, conv_w[layer], conv_b[layer], w_br_conv[layer].astype(BF16))

    cos_t, sin_t = _rope_tables()
    dl = ret_decay_logit[layer].astype(F32)
    r_ctx, new_state = _retention(proj, dl, ret_gn_w[layer], None, None, None, latent=False)
    r_lat = _retention(proj, dl, ret_gn_w[layer], cos_t, sin_t, state_ret, latent=True)

    x1, h2, top_i, top_w = _mix_route(
        r_ctx, r_lat, w_br_ret[layer].astype(BF16), a, proj, w_out[layer].astype(BF16),
        x_ctx, x_lat, mod3, norm_mix_post[layer], norm_ffn_pre[layer],
        router_w[layer].T.astype(BF16), router_b[layer])

    slot, tok_of_slot, n_tiles, x_tile, o_tile, n_valid, g_exp = _routing_tables(top_i)
    xs = _dispatch(h2, tok_of_slot, n_tiles)
    ys = _moe(xs, moe_w_gate[layer], moe_w_up[layer], moe_w_down[layer],
              moe_b_gate[layer], moe_b_up[layer], moe_b_down[layer],
              x_tile, o_tile, n_valid, g_exp)
    y_ctx, y_lat = _combine(ys, slot, top_w.T, x1, mod3, norm_ffn_post[layer])

    return (y_ctx.reshape(BATCH, SEQ, D), y_lat.reshape(DEC_BATCH, DEC_SEQ, D), new_state)
```

```python
import functools

import jax
import jax.numpy as jnp
import numpy as np
from jax import lax
from jax.experimental import pallas as pl
from jax.experimental.pallas import tpu as pltpu

F32 = jnp.float32
BF16 = jnp.bfloat16
I32 = jnp.int32

D = 2048
BATCH = 16
SEQ = 256
DEC_BATCH = 2
DEC_SEQ = 1024
GRID_W = 64
CONV_W = 1024
HEADS = 8
DK = 128
DV = 256
CHUNK = 128
N_EXP = 32
TOP_K = 4
D_FF = 2048
LIMIT = 7.0
ALPHA = 1.702
THETA = 10000.0
EPS = 1e-6
N_MOD = 6
N_CTX = BATCH * SEQ
N_LAT = DEC_BATCH * DEC_SEQ
N_TOK = N_CTX + N_LAT
IN_COLS = 3 * CONV_W + 2 * HEADS * DK + 2 * HEADS * DV + 2 * D
COL_CX, COL_CB, COL_CC = 0, CONV_W, 2 * CONV_W
COL_Q = 3 * CONV_W
COL_K = COL_Q + HEADS * DK
COL_V = COL_K + HEADS * DK
COL_G = COL_V + HEADS * DV
COL_MGC = COL_G + HEADS * DV
COL_MGR = COL_MGC + D
MOD_SH1, MOD_SC1, MOD_G1, MOD_SH2, MOD_SC2, MOD_G2 = range(6)
MOD_ROWS = 8

MOE_TM = 256
N_PAIR = N_TOK * TOP_K
MOE_TILES = (N_PAIR + N_EXP * (MOE_TM - 1)) // MOE_TM + 1
MOE_ROWS = MOE_TILES * MOE_TM
MOE_SUB = 4
MOE_GROUPS = (MOE_TILES + (MOE_SUB - 1) * N_EXP) // MOE_SUB
MOE_CH = 512
MOE_NCH = D_FF // MOE_CH

MIB = 1024 * 1024


def _cparams(sem, vmem_mib):
    return pltpu.CompilerParams(dimension_semantics=sem, vmem_limit_bytes=vmem_mib * MIB)


def _mod_row(t0):
    return jnp.where(t0 < N_CTX, 0, 1 + (t0 - N_CTX) // DEC_SEQ)


def _mod_spec(tm, chunk):
    return pl.BlockSpec((None, 1, D), lambda i: (_mod_row(i * tm), 0, chunk))


def _rms(x, g):
    return (x * lax.rsqrt(jnp.mean(x * x, axis=-1, keepdims=True) + EPS)) * g


def _ada_kernel(c_ref, w_ref, b_ref, o_ref):
    c = c_ref[...]
    s = c * jax.nn.sigmoid(c)
    o_ref[...] = jnp.dot(s.astype(BF16), w_ref[...].astype(BF16),
                         preferred_element_type=F32) + b_ref[...]


def _modulation(cvec, w_ada, b_ada):
    tn = 1024
    return pl.pallas_call(
        _ada_kernel,
        grid=(N_MOD * D // tn,),
        in_specs=[pl.BlockSpec((MOD_ROWS, D), lambda j: (0, 0)),
                  pl.BlockSpec((D, tn), lambda j: (0, j)),
                  pl.BlockSpec((1, tn), lambda j: (0, j))],
        out_specs=pl.BlockSpec((MOD_ROWS, tn), lambda j: (0, j)),
        out_shape=jax.ShapeDtypeStruct((MOD_ROWS, N_MOD * D), F32),
        compiler_params=_cparams(("arbitrary",), 40),
        name="ada_modulation",
    )(cvec, w_ada, b_ada.reshape(1, -1))


NORM_TM = 512


def _ctx_lat_specs(tm, width):
    ctx_tiles = N_CTX // tm
    return (pl.BlockSpec((tm, width), lambda i, *_: (jnp.minimum(i, ctx_tiles - 1), 0)),
            pl.BlockSpec((tm, width), lambda i, *_: (jnp.maximum(i - ctx_tiles, 0), 0)))


def _norm_mod_kernel(xc_ref, xl_ref, g_ref, sh_ref, sc_ref, o_ref):
    x = jnp.where(pl.program_id(0) * NORM_TM < N_CTX, xc_ref[...], xl_ref[...])
    y = _rms(x, g_ref[...])
    o_ref[...] = (y * (1.0 + sc_ref[...]) + sh_ref[...]).astype(BF16)


def _norm_mod(x_ctx, x_lat, g, mod3):
    tm = NORM_TM
    return pl.pallas_call(
        _norm_mod_kernel,
        grid=(N_TOK // tm,),
        in_specs=[*_ctx_lat_specs(tm, D),
                  pl.BlockSpec((1, D), lambda i: (0, 0)),
                  _mod_spec(tm, MOD_SH1),
                  _mod_spec(tm, MOD_SC1)],
        out_specs=pl.BlockSpec((tm, D), lambda i: (i, 0)),
        out_shape=jax.ShapeDtypeStruct((N_TOK, D), BF16),
        compiler_params=_cparams(("arbitrary",), 40),
        name="norm_modulate",
    )(x_ctx, x_lat, g.reshape(1, D), mod3, mod3)


def _proj_kernel(h_ref, w_ref, o_ref, wb_ref):
    @pl.when(pl.program_id(1) == 0)
    def _():
        wb_ref[...] = w_ref[...].astype(BF16)

    o_ref[...] = jnp.dot(h_ref[...], wb_ref[...], preferred_element_type=F32).astype(BF16)


def _in_proj(h, w_in):
    tm, tn = 1024, 1024
    return pl.pallas_call(
        _proj_kernel,
        grid=(IN_COLS // tn, N_TOK // tm),
        in_specs=[pl.BlockSpec((tm, D), lambda n, m: (m, 0)),
                  pl.BlockSpec((D, tn), lambda n, m: (0, n))],
        out_specs=pl.BlockSpec((tm, tn), lambda n, m: (m, n)),
        out_shape=jax.ShapeDtypeStruct((N_TOK, IN_COLS), BF16),
        scratch_shapes=[pltpu.VMEM((D, tn), BF16)],
        compiler_params=_cparams(("arbitrary", "arbitrary"), 48),
        name="in_proj",
    )(h, w_in)


CONV_TM = 256


def _conv_kernel(cx_ref, cb_ref, cc_ref, mg0_ref, mg1_ref, cw_ref, cbias_ref, w_ref, o_ref):
    i = pl.program_id(0)
    u = cc_ref[...].astype(F32) * cx_ref[...].astype(F32)
    seg = jnp.where(i * CONV_TM < N_CTX, SEQ, GRID_W)
    pos = lax.broadcasted_iota(I32, (CONV_TM, 1), 0) & (seg - 1)
    prev = jnp.where(pos == 0, 0.0, pltpu.roll(u, 1, 0))
    nxt = jnp.where(pos == seg - 1, 0.0, pltpu.roll(u, CONV_TM - 1, 0))
    cw = cw_ref[...]
    uc = prev * cw[0:1, :] + u * cw[1:2, :] + nxt * cw[2:3, :] + cbias_ref[...]
    z = (cb_ref[...].astype(F32) * uc).astype(BF16)
    y = jnp.dot(z, w_ref[...], preferred_element_type=F32)
    mg = jnp.concatenate([mg0_ref[...], mg1_ref[...]], axis=1).astype(F32)
    o_ref[...] = (jax.nn.sigmoid(mg) * y).astype(BF16)


def _conv_branch(proj, conv_w, conv_b, w_br_conv_bf):
    tm = CONV_TM
    cw = 1024

    def col(c):
        return pl.BlockSpec((tm, cw), lambda i: (i, c))

    return pl.pallas_call(
        _conv_kernel,
        grid=(N_TOK // tm,),
        in_specs=[col(COL_CX // cw), col(COL_CB // cw), col(COL_CC // cw),
                  col(COL_MGC // cw), col(COL_MGC // cw + 1),
                  pl.BlockSpec((3, CONV_W), lambda i: (0, 0)),
                  pl.BlockSpec((1, CONV_W), lambda i: (0, 0)),
                  pl.BlockSpec((CONV_W, D), lambda i: (0, 0))],
        out_specs=pl.BlockSpec((tm, D), lambda i: (i, 0)),
        out_shape=jax.ShapeDtypeStruct((N_TOK, D), BF16),
        compiler_params=_cparams(("arbitrary",), 48),
        name="conv_branch",
    )(proj, proj, proj, proj, proj, conv_w, conv_b.reshape(1, CONV_W), w_br_conv_bf)


def _log_sigmoid(x):
    return -(jnp.maximum(-x, 0.0) + jnp.log1p(jnp.exp(-jnp.abs(x))))


def _rope(x, cos, sin_signed):
    lane = lax.broadcasted_iota(I32, x.shape, 1)
    swapped = jnp.where((lane & 63) < 32, pltpu.roll(x, 96, 1), pltpu.roll(x, 32, 1))
    return x * cos + swapped * sin_signed


def _retention_kernel(*refs, seq_len, latent):
    if latent:
        (dl_ref, q_ref, k_ref, v_ref, g_ref, gn_ref, cos_ref, sin_ref, s0_ref,
         r_ref, o_acc) = refs
    else:
        dl_ref, q_ref, k_ref, v_ref, g_ref, gn_ref, r_ref, sout_ref, o_acc = refs
    h = pl.program_id(1)
    c = CHUNK
    n_chunks = seq_len // c

    lgf = _log_sigmoid(jnp.full((c, c), dl_ref[0, h], F32))
    lgb = _log_sigmoid(jnp.full((c, c), dl_ref[1, h], F32))
    lgf_w = _log_sigmoid(jnp.full((c, DV), dl_ref[0, h], F32))
    lgb_w = _log_sigmoid(jnp.full((c, DV), dl_ref[1, h], F32))

    ii = lax.broadcasted_iota(I32, (c, c), 0).astype(F32)
    jj = lax.broadcasted_iota(I32, (c, c), 1).astype(F32)
    diff = ii - jj
    m_in = jnp.where(diff >= 0.0, jnp.exp(jnp.maximum(diff, 0.0) * lgf),
                     jnp.exp(jnp.maximum(-diff, 0.0) * lgb))
    io = lax.broadcasted_iota(I32, (c, DV), 0).astype(F32)
    dq_f = jnp.exp((io + 1.0) * lgf_w)
    dq_b = jnp.exp((c - io) * lgb_w)
    dk_f = jnp.exp((c - 1.0 - jj) * lgf)
    dk_b = jnp.exp(jj * lgb)
    dc_f = jnp.exp(c * lgf_w)
    dc_b = jnp.exp(c * lgb_w)

    q = q_ref[...].astype(F32)
    k = k_ref[...].astype(F32) * (DK ** -0.5)
    if latent:
        q = _rope(q, cos_ref[...], sin_ref[...])
        k = _rope(k, cos_ref[...], sin_ref[...])
        s_f = s0_ref[0]
        s_b = s0_ref[1]
    else:
        s_f = jnp.zeros((DK, DV), F32)
        s_b = jnp.zeros((DK, DV), F32)
    qb = q.astype(BF16)

    def chunk(x, n):
        return x[n * c:(n + 1) * c]

    for n in range(n_chunks):
        qn = chunk(qb, n)
        kt = chunk(k, n).T
        vn = v_ref[n * c:(n + 1) * c, :]
        s = jnp.dot(qn, kt.astype(BF16), preferred_element_type=F32)
        p = (s * m_in).astype(BF16)
        o = jnp.dot(p, vn, preferred_element_type=F32)
        o = o + jnp.dot(qn, s_f.astype(BF16), preferred_element_type=F32) * dq_f
        o_acc[n * c:(n + 1) * c, :] = o
        s_f = dc_f * s_f + jnp.dot((kt * dk_f).astype(BF16), vn, preferred_element_type=F32)

    for n in reversed(range(n_chunks)):
        qn = chunk(qb, n)
        kt = chunk(k, n).T
        vn = v_ref[n * c:(n + 1) * c, :]
        o_acc[n * c:(n + 1) * c, :] += (
            jnp.dot(qn, s_b.astype(BF16), preferred_element_type=F32) * dq_b)
        s_b = dc_b * s_b + jnp.dot((kt * dk_b).astype(BF16), vn, preferred_element_type=F32)

    if not latent:
        sout_ref[0] = s_f
        sout_ref[1] = s_b

    o = o_acc[...]
    mu = jnp.mean(o, axis=-1, keepdims=True)
    oc = o - mu
    var = jnp.mean(oc * oc, axis=-1, keepdims=True)
    on = oc * lax.rsqrt(var + EPS) * gn_ref[...]
    g = g_ref[...].astype(F32)
    r_ref[...] = ((g * jax.nn.sigmoid(g)) * on).astype(BF16)


def _retention(proj, decay_logit, gn_w, cos_t, sin_t, state0, *, latent):
    seq_len = DEC_SEQ if latent else SEQ
    nb = DEC_BATCH if latent else BATCH
    rb0 = (N_CTX // DEC_SEQ) if latent else 0

    def spec(width, col0):
        return pl.BlockSpec((seq_len, width), lambda b, h, dl: (rb0 + b, col0 // width + h))

    in_specs = [spec(DK, COL_Q), spec(DK, COL_K), spec(DV, COL_V), spec(DV, COL_G),
                pl.BlockSpec((1, DV), lambda b, h, dl: (0, h))]
    args = [proj, proj, proj, proj, gn_w.reshape(1, HEADS * DV)]
    state_spec = pl.BlockSpec((None, None, 2, None, DK, DV), lambda b, h, dl: (b, 0, 0, h, 0, 0))
    r_shape = jax.ShapeDtypeStruct((nb * seq_len, HEADS * DV), BF16)
    r_spec = pl.BlockSpec((seq_len, DV), lambda b, h, dl: (b, h))
    if latent:
        in_specs += [pl.BlockSpec((seq_len, DK), lambda b, h, dl: (0, 0)),
                     pl.BlockSpec((seq_len, DK), lambda b, h, dl: (0, 0)),
                     state_spec]
        args += [cos_t, sin_t, state0]
        out_specs, out_shape = r_spec, r_shape
    else:
        out_specs = (r_spec, state_spec)
        out_shape = (r_shape, jax.ShapeDtypeStruct((BATCH, 1, 2, HEADS, DK, DV), F32))
    return pl.pallas_call(
        functools.partial(_retention_kernel, seq_len=seq_len, latent=latent),
        grid_spec=pltpu.PrefetchScalarGridSpec(
            num_scalar_prefetch=1, grid=(nb, HEADS), in_specs=in_specs, out_specs=out_specs,
            scratch_shapes=[pltpu.VMEM((seq_len, DV), F32)]),
        out_shape=out_shape,
        compiler_params=_cparams(("arbitrary", "arbitrary"), 40),
        name="retention_latent" if latent else "retention_context",
    )(decay_logit, *args)


MIX_TM = 256


def _mix_kernel(rc_ref, rl_ref, wbr_ref, a_ref, mg0_ref, mg1_ref, wout_ref, xc_ref, xl_ref, g1_ref,
                sc2_ref, sh2_ref, npost_ref, npre_ref, rwt_ref, rb_ref,
                x1_ref, h2_ref, ti_ref, tw_ref):
    is_ctx = pl.program_id(0) * MIX_TM < N_CTX
    r = jnp.where(is_ctx, rc_ref[...], rl_ref[...])
    x = jnp.where(is_ctx, xc_ref[...], xl_ref[...])
    y = jnp.dot(r, wbr_ref[...], preferred_element_type=F32)
    mg = jnp.concatenate([mg0_ref[...], mg1_ref[...]], axis=1).astype(F32)
    mix = a_ref[...].astype(F32) + jax.nn.sigmoid(mg) * y
    out = jnp.dot(mix.astype(BF16), wout_ref[...], preferred_element_type=F32)
    x1 = x + g1_ref[...] * _rms(out, npost_ref[...])
    x1_ref[...] = x1
    h2 = _rms(x1, npre_ref[...]) * (1.0 + sc2_ref[...]) + sh2_ref[...]
    h2_ref[...] = h2
    logits = lax.dot_general(rwt_ref[...], h2.astype(BF16), (((1,), (1,)), ((), ())),
                             preferred_element_type=F32) + rb_ref[...]
    eidx = lax.broadcasted_iota(I32, logits.shape, 0).astype(F32)
    vals, idxs = [], []
    for _ in range(TOP_K):
        m = jnp.max(logits, axis=0, keepdims=True)
        am = jnp.min(jnp.where(logits == m, eidx, float(N_EXP)), axis=0, keepdims=True)
        vals.append(m)
        idxs.append(am)
        logits = jnp.where(eidx == am, -jnp.inf, logits)
    es = [jnp.exp(v - vals[0]) for v in vals]
    denom = es[0] + es[1] + es[2] + es[3]
    for kk in range(TOP_K):
        ti_ref[kk:kk + 1, :] = idxs[kk].astype(I32)
        tw_ref[kk:kk + 1, :] = es[kk] / denom


def _mix_route(r_ctx, r_lat, wbr_bf, a, proj, wout_bf, x_ctx, x_lat, mod3, npost, npre, rwt_bf, rb):
    tm = MIX_TM
    cw = 1024

    def full(shape):
        return pl.BlockSpec(shape, lambda i: (0,) * len(shape), pipeline_mode=pl.Buffered(1))

    def row(width):
        return pl.BlockSpec((tm, width), lambda i: (i, 0))

    return pl.pallas_call(
        _mix_kernel,
        grid=(N_TOK // tm,),
        in_specs=[*_ctx_lat_specs(tm, D),
                  full((D, D)), row(D),
                  pl.BlockSpec((tm, cw), lambda i: (i, COL_MGR // cw)),
                  pl.BlockSpec((tm, cw), lambda i: (i, COL_MGR // cw + 1)),
                  full((D, D)), *_ctx_lat_specs(tm, D),
                  _mod_spec(tm, MOD_G1), _mod_spec(tm, MOD_SC2), _mod_spec(tm, MOD_SH2),
                  full((1, D)), full((1, D)), full((N_EXP, D)), full((N_EXP, 1))],
        out_specs=(row(D), row(D),
                   pl.BlockSpec((TOP_K, tm), lambda i: (0, i)),
                   pl.BlockSpec((TOP_K, tm), lambda i: (0, i))),
        out_shape=(jax.ShapeDtypeStruct((N_TOK, D), F32),
                   jax.ShapeDtypeStruct((N_TOK, D), F32),
                   jax.ShapeDtypeStruct((TOP_K, N_TOK), I32),
                   jax.ShapeDtypeStruct((TOP_K, N_TOK), F32)),
        compiler_params=_cparams(("arbitrary",), 56),
        name="mix_out_route",
    )(r_ctx, r_lat, wbr_bf, a, proj, proj, wout_bf, x_ctx, x_lat, mod3, mod3, mod3,
      npost.reshape(1, D), npre.reshape(1, D), rwt_bf, rb.reshape(N_EXP, 1))


def _routing_tables(top_i):
    e = top_i.reshape(-1)
    onehot = e[:, None] == jnp.arange(N_EXP, dtype=I32)[None, :]
    blk = 256
    oh = onehot.astype(F32).reshape(N_PAIR // blk, blk, N_EXP)
    tri = (jnp.arange(blk)[:, None] >= jnp.arange(blk)[None, :]).astype(F32)
    within = jnp.einsum("ij,bjk->bik", tri, oh)
    blk_tot = within[:, -1, :]
    blk_off = jnp.cumsum(blk_tot, axis=0) - blk_tot
    csum = (within + blk_off[:, None, :]).reshape(N_PAIR, N_EXP)
    rank = jnp.sum(jnp.where(onehot, csum, 0.0), axis=1).astype(I32) - 1
    counts = csum[-1].astype(I32)
    tiles_e = (counts + MOE_TM - 1) // MOE_TM
    tile_end = jnp.cumsum(tiles_e)
    tile_start = tile_end - tiles_e
    slot = tile_start[e] * MOE_TM + rank
    n_tiles = tile_end[-1]
    tok = jnp.tile(jnp.arange(N_TOK, dtype=I32), TOP_K)
    tok_of_slot = jnp.zeros((MOE_ROWS,), I32).at[slot].set(tok, unique_indices=True)

    groups_e = (tiles_e + MOE_SUB - 1) // MOE_SUB
    g_end = jnp.cumsum(groups_e)
    g_start = g_end - groups_e
    n_groups = g_end[-1]
    gid = jnp.arange(MOE_GROUPS, dtype=I32)
    used = gid < n_groups
    g_exp = jnp.minimum(jnp.sum((gid[:, None] >= g_end[None, :]).astype(I32), axis=1), N_EXP - 1)
    g_exp = jnp.where(used, g_exp, g_exp[jnp.maximum(n_groups - 1, 0)])
    g_in_e = gid - g_start[g_exp]
    first = tile_start[g_exp] + MOE_SUB * g_in_e
    n_valid = jnp.where(used, jnp.clip(tiles_e[g_exp] - MOE_SUB * g_in_e, 0, MOE_SUB), 0)
    sub = jnp.arange(MOE_SUB, dtype=I32)[None, :]
    x_tile = first[:, None] + jnp.minimum(sub, jnp.maximum(n_valid[:, None] - 1, 0))
    x_tile = jnp.where(used[:, None], x_tile, x_tile[jnp.maximum(n_groups - 1, 0)][None, :])
    tail = n_tiles + (gid[:, None] - n_groups) * MOE_SUB + sub
    o_tile = jnp.where(used[:, None],
                       jnp.where(sub < n_valid[:, None], first[:, None] + sub, -1),
                       jnp.where(tail < MOE_TILES, tail, -1))
    return (slot, tok_of_slot, n_tiles.reshape(1).astype(I32), x_tile.reshape(-1).astype(I32),
            o_tile.reshape(-1).astype(I32), n_valid.astype(I32), g_exp.astype(I32))


def _row_copy(src_hbm, row, dst_ref, r, sem):
    return pltpu.make_async_copy(src_hbm.at[pl.ds(row, 1), :], dst_ref.at[pl.ds(r, 1), :], sem)


DISP_RING = 4
DISP_LAG = DISP_RING - 1


def _dispatch_kernel(nt_ref, tos_ref, x_hbm, o_ref, buf, sem):
    j = pl.program_id(0)
    nt = nt_ref[0]
    done = j - DISP_LAG

    @pl.when(j < nt)
    def _():
        b = j & (DISP_RING - 1)

        def issue(r, carry):
            _row_copy(x_hbm, tos_ref[0, r], buf.at[b], r, sem.at[b]).start()
            return carry

        lax.fori_loop(0, MOE_TM, issue, 0, unroll=8)

    @pl.when(jnp.logical_and(done >= 0, done < nt))
    def _():
        b = done & (DISP_RING - 1)

        def wait(r, carry):
            _row_copy(x_hbm, 0, buf.at[b], r, sem.at[b]).wait()
            return carry

        lax.fori_loop(0, MOE_TM, wait, 0, unroll=8)
        o_ref[...] = buf[b].astype(BF16)

    @pl.when(jnp.logical_and(done >= 0, done >= nt))
    def _():
        o_ref[...] = jnp.zeros_like(o_ref)


def _dispatch(h2, tok_of_slot, n_tiles):
    return pl.pallas_call(
        _dispatch_kernel,
        grid_spec=pltpu.PrefetchScalarGridSpec(
            num_scalar_prefetch=1, grid=(MOE_TILES + DISP_LAG,),
            in_specs=[pl.BlockSpec((None, 1, MOE_TM),
                                   lambda j, nt: (jnp.minimum(j, MOE_TILES - 1), 0, 0),
                                   memory_space=pltpu.SMEM),
                      pl.BlockSpec(memory_space=pl.ANY)],
            out_specs=pl.BlockSpec((MOE_TM, D), lambda j, nt: (jnp.maximum(j - DISP_LAG, 0), 0)),
            scratch_shapes=[pltpu.VMEM((DISP_RING, MOE_TM, D), F32),
                            pltpu.SemaphoreType.DMA((DISP_RING,))]),
        out_shape=jax.ShapeDtypeStruct((MOE_ROWS, D), BF16),
        compiler_params=_cparams(("arbitrary",), 32),
        name="moe_dispatch",
    )(n_tiles, tok_of_slot.reshape(MOE_TILES, 1, MOE_TM), h2)


def _tile_store(acc_ref, b, i, y_hbm, tile, sem):
    return pltpu.make_async_copy(acc_ref.at[b, i], y_hbm.at[pl.ds(tile * MOE_TM, MOE_TM), :],
                                 sem.at[b])


def _moe_kernel(xt_ref, ot_ref, nv_ref, ge_ref, *refs):
    x_refs = refs[:MOE_SUB]
    wg_ref, wu_ref, wd_ref, bg_ref, bu_ref, bd_ref, y_hbm = refs[MOE_SUB:MOE_SUB + 7]
    wgb_ref, wub_ref, wdb_ref, acc_ref, sem = refs[MOE_SUB + 7:]
    g = pl.program_id(0)
    c = pl.program_id(1)
    nv = nv_ref[g]
    b = g & 1
    last_chunk = c == MOE_NCH - 1

    def wait_group(gw):
        for i in range(MOE_SUB):
            @pl.when(ot_ref[gw * MOE_SUB + i] >= 0)
            def _(i=i):
                _tile_store(acc_ref, gw & 1, i, y_hbm, 0, sem).wait()

    @pl.when(jnp.logical_and(c == 0, g >= 2))
    def _():
        wait_group(g - 2)

    @pl.when(jnp.logical_and(c == 0, nv > 0))
    def _():
        for i in range(MOE_SUB):
            acc_ref[b, i] = jnp.broadcast_to(bd_ref[...], (MOE_TM, D))

    def group_step(m):
        wgb_ref[...] = wg_ref[...].astype(BF16)
        wub_ref[...] = wu_ref[...].astype(BF16)
        wdb_ref[...] = wd_ref[...].astype(BF16)
        for i in range(m):
            x = x_refs[i][...]
            gate = jnp.dot(x, wgb_ref[...], preferred_element_type=F32) + bg_ref[...]
            up = jnp.dot(x, wub_ref[...], preferred_element_type=F32) + bu_ref[...]
            gate = jnp.minimum(gate, LIMIT)
            up = jnp.clip(up, -LIMIT, LIMIT)
            act = (gate * jax.nn.sigmoid(ALPHA * gate) * (up + 1.0)).astype(BF16)
            acc_ref[b, i] += jnp.dot(act, wdb_ref[...], preferred_element_type=F32)

    for m in range(1, MOE_SUB + 1):
        pl.when(nv == m)(functools.partial(group_step, m))

    for i in range(MOE_SUB):
        @pl.when(jnp.logical_and(jnp.logical_and(i >= nv, last_chunk), ot_ref[g * MOE_SUB + i] >= 0))
        def _(i=i):
            acc_ref[b, i] = jnp.zeros((MOE_TM, D), F32)

        @pl.when(jnp.logical_and(last_chunk, ot_ref[g * MOE_SUB + i] >= 0))
        def _(i=i):
            _tile_store(acc_ref, b, i, y_hbm, ot_ref[g * MOE_SUB + i], sem).start()

    @pl.when(jnp.logical_and(last_chunk, g == MOE_GROUPS - 1))
    def _():
        wait_group(g - 1)
        wait_group(g)


def _moe(xs, wg, wu, wd, bg, bu, bd, x_tile, o_tile, n_valid, g_exp):
    def chunk_of(g, c, nv):
        return jnp.where(nv[g] > 0, c, MOE_NCH - 1)

    def x_spec(i):
        return pl.BlockSpec((MOE_TM, D), lambda g, c, xt, ot, nv, ge: (xt[g * MOE_SUB + i], 0))

    w_in_spec = pl.BlockSpec((None, D, MOE_CH),
                             lambda g, c, xt, ot, nv, ge: (ge[g], 0, chunk_of(g, c, nv)))
    b_in_spec = pl.BlockSpec((None, 1, MOE_CH),
                             lambda g, c, xt, ot, nv, ge: (ge[g], 0, chunk_of(g, c, nv)))
    w_out_spec = pl.BlockSpec((None, MOE_CH, D),
                              lambda g, c, xt, ot, nv, ge: (ge[g], chunk_of(g, c, nv), 0))
    b_out_spec = pl.BlockSpec((None, 1, D), lambda g, c, xt, ot, nv, ge: (ge[g], 0, 0))
    return pl.pallas_call(
        _moe_kernel,
        grid_spec=pltpu.PrefetchScalarGridSpec(
            num_scalar_prefetch=4, grid=(MOE_GROUPS, MOE_NCH),
            in_specs=[*[x_spec(i) for i in range(MOE_SUB)],
                      w_in_spec, w_in_spec, w_out_spec, b_in_spec, b_in_spec, b_out_spec],
            out_specs=pl.BlockSpec(memory_space=pl.ANY),
            scratch_shapes=[pltpu.VMEM((D, MOE_CH), BF16), pltpu.VMEM((D, MOE_CH), BF16),
                            pltpu.VMEM((MOE_CH, D), BF16),
                            pltpu.VMEM((2, MOE_SUB, MOE_TM, D), F32),
                            pltpu.SemaphoreType.DMA((2,))]),
        out_shape=jax.ShapeDtypeStruct((MOE_ROWS, D), F32),
        compiler_params=_cparams(("arbitrary", "arbitrary"), 62),
        name="moe_experts",
    )(x_tile, o_tile, n_valid, g_exp, *([xs] * MOE_SUB), wg, wu, wd,
      bg.reshape(N_EXP, 1, D_FF), bu.reshape(N_EXP, 1, D_FF), bd.reshape(N_EXP, 1, D))


COMB_TM = 256


COMB_TILES = N_TOK // COMB_TM


def _combine_kernel(slot_ref, ys_hbm, w_ref, x1_ref, g2_ref, npost_ref, oc_ref, ol_ref, buf, sem):
    i = pl.program_id(0)

    @pl.when(i < COMB_TILES)
    def _():
        b = i & 1
        for kk in range(TOP_K):
            def issue(r, carry, kk=kk):
                _row_copy(ys_hbm, slot_ref[kk, r], buf.at[b, kk], r, sem.at[b]).start()
                return carry

            lax.fori_loop(0, COMB_TM, issue, 0, unroll=8)

    @pl.when(i >= 1)
    def _():
        b = (i - 1) & 1
        for kk in range(TOP_K):
            def wait(r, carry, kk=kk):
                _row_copy(ys_hbm, 0, buf.at[b, kk], r, sem.at[b]).wait()
                return carry

            lax.fori_loop(0, COMB_TM, wait, 0, unroll=8)
        w = w_ref[...]
        f = buf[b, 0] * w[:, 0:1]
        for kk in range(1, TOP_K):
            f = f + buf[b, kk] * w[:, kk:kk + 1]
        out = x1_ref[...] + g2_ref[...] * _rms(f, npost_ref[...])
        base = (i - 1) * COMB_TM

        @pl.when(base < N_CTX)
        def _():
            oc_ref[...] = out

        @pl.when(base >= N_CTX)
        def _():
            ol_ref[...] = out


def _combine(ys, slot, top_w_t, x1, mod3, npost):
    tm = COMB_TM

    def prev(i):
        return jnp.maximum(i - 1, 0)

    oc_spec, ol_spec = _ctx_lat_specs(tm, D)
    return pl.pallas_call(
        _combine_kernel,
        grid=(COMB_TILES + 1,),
        in_specs=[pl.BlockSpec((TOP_K, tm), lambda i: (0, jnp.minimum(i, COMB_TILES - 1)),
                               memory_space=pltpu.SMEM),
                  pl.BlockSpec(memory_space=pl.ANY),
                  pl.BlockSpec((tm, TOP_K), lambda i: (prev(i), 0)),
                  pl.BlockSpec((tm, D), lambda i: (prev(i), 0)),
                  pl.BlockSpec((None, 1, D), lambda i: (_mod_row(prev(i) * tm), 0, MOD_G2)),
                  pl.BlockSpec((1, D), lambda i: (0, 0))],
        out_specs=(pl.BlockSpec((tm, D), lambda i: oc_spec.index_map(prev(i))),
                   pl.BlockSpec((tm, D), lambda i: ol_spec.index_map(prev(i)))),
        scratch_shapes=[pltpu.VMEM((2, TOP_K, tm, D), F32), pltpu.SemaphoreType.DMA((2,))],
        out_shape=(jax.ShapeDtypeStruct((N_CTX, D), F32), jax.ShapeDtypeStruct((N_LAT, D), F32)),
        compiler_params=_cparams(("arbitrary",), 48),
        name="moe_combine",
    )(slot.reshape(TOP_K, N_TOK), ys, top_w_t, x1, mod3, npost.reshape(1, D))


def _rope_tables():
    pos = jnp.arange(DEC_SEQ)
    row = (pos // GRID_W).astype(F32)
    col = (pos % GRID_W).astype(F32)
    n_freq = DK // 4
    freqs = THETA ** (-jnp.arange(n_freq, dtype=F32) / n_freq)
    ar = row[:, None] * freqs
    ac = col[:, None] * freqs
    cos_t = jnp.concatenate([jnp.cos(ar), jnp.cos(ar), jnp.cos(ac), jnp.cos(ac)], axis=1)
    sin_t = jnp.concatenate([-jnp.sin(ar), jnp.sin(ar), -jnp.sin(ac), jnp.sin(ac)], axis=1)
    return cos_t.astype(F32), sin_t.astype(F32)


def kernel(x_prompt, x_sample, state_ret, c, c_ctx, w_ada, b_ada, norm_mix_pre, norm_mix_post,
           norm_ffn_pre, norm_ffn_post, w_in, conv_w, conv_b, ret_decay_logit, ret_gn_w,
           w_br_conv, w_br_ret, w_out, router_w, router_b, moe_w_gate, moe_b_gate,
           moe_w_up, moe_b_up, moe_w_down, moe_b_down):
    layer = 0
    x_ctx = x_prompt.reshape(N_CTX, D)
    x_lat = x_sample.reshape(N_LAT, D)
    cvec = jnp.concatenate(
        [c_ctx[None, :], c, jnp.zeros((MOD_ROWS - 1 - DEC_BATCH, D), F32)], axis=0)

    mod = _modulation(cvec, w_ada[layer], b_ada[layer])
    mod3 = mod.reshape(MOD_ROWS, 1, N_MOD * D)

    h1 = _norm_mod(x_ctx, x_lat, norm_mix_pre[layer], mod3)
    proj = _in_proj(h1, w_in[layer])

    a = _conv_branch(proj, conv_w[layer], conv_b[layer], w_br_conv[layer].astype(BF16))

    cos_t, sin_t = _rope_tables()
    dl = ret_decay_logit[layer].astype(F32)
    r_ctx, new_state = _retention(proj, dl, ret_gn_w[layer], None, None, None, latent=False)
    r_lat = _retention(proj, dl, ret_gn_w[layer], cos_t, sin_t, state_ret, latent=True)

    x1, h2, top_i, top_w = _mix_route(
        r_ctx, r_lat, w_br_ret[layer].astype(BF16), a, proj, w_out[layer].astype(BF16),
        x_ctx, x_lat, mod3, norm_mix_post[layer], norm_ffn_pre[layer],
        router_w[layer].T.astype(BF16), router_b[layer])

    slot, tok_of_slot, n_tiles, x_tile, o_tile, n_valid, g_exp = _routing_tables(top_i)
    xs = _dispatch(h2, tok_of_slot, n_tiles)
    ys = _moe(xs, moe_w_gate[layer], moe_w_up[layer], moe_w_down[layer],
              moe_b_gate[layer], moe_b_up[layer], moe_b_down[layer],
              x_tile, o_tile, n_valid, g_exp)
    y_ctx, y_lat = _combine(ys, slot, top_w.T, x1, mod3, norm_ffn_post[layer])

    return (y_ctx.reshape(BATCH, SEQ, D), y_lat.reshape(DEC_BATCH, DEC_SEQ, D), new_state)
```

```python
import functools

import jax
import jax.numpy as jnp
import numpy as np
from jax import lax
from jax.experimental import pallas as pl
from jax.experimental.pallas import tpu as pltpu

F32 = jnp.float32
BF16 = jnp.bfloat16
I32 = jnp.int32

D = 2048
BATCH = 16
SEQ = 256
DEC_BATCH = 2
DEC_SEQ = 1024
GRID_W = 64
CONV_W = 1024
HEADS = 8
DK = 128
DV = 256
CHUNK = 128
N_EXP = 32
TOP_K = 4
D_FF = 2048
LIMIT = 7.0
ALPHA = 1.702
THETA = 10000.0
EPS = 1e-6
N_MOD = 6
N_CTX = BATCH * SEQ
N_LAT = DEC_BATCH * DEC_SEQ
N_TOK = N_CTX + N_LAT
IN_COLS = 3 * CONV_W + 2 * HEADS * DK + 2 * HEADS * DV + 2 * D
COL_CX, COL_CB, COL_CC = 0, CONV_W, 2 * CONV_W
COL_Q = 3 * CONV_W
COL_K = COL_Q + HEADS * DK
COL_V = COL_K + HEADS * DK
COL_G = COL_V + HEADS * DV
COL_MGC = COL_G + HEADS * DV
COL_MGR = COL_MGC + D
MOD_SH1, MOD_SC1, MOD_G1, MOD_SH2, MOD_SC2, MOD_G2 = range(6)
MOD_ROWS = 8

MOE_TM = 256
N_PAIR = N_TOK * TOP_K
MOE_TILES = (N_PAIR + N_EXP * (MOE_TM - 1)) // MOE_TM + 1
MOE_ROWS = MOE_TILES * MOE_TM
MOE_SUB = 4
MOE_GROUPS = (MOE_TILES + (MOE_SUB - 1) * N_EXP) // MOE_SUB
MOE_CH = 512
MOE_NCH = D_FF // MOE_CH

LANE_CHUNKS = D // 128
MIB = 1024 * 1024


def _cparams(sem, vmem_mib):
    return pltpu.CompilerParams(dimension_semantics=sem, vmem_limit_bytes=vmem_mib * MIB)


def _mod_row(t0):
    return jnp.where(t0 < N_CTX, 0, 1 + (t0 - N_CTX) // DEC_SEQ)


def _mod_spec(tm, chunk):
    return pl.BlockSpec((None, 1, D), lambda i: (_mod_row(i * tm), 0, chunk))


def _rms(x, g):
    return (x * lax.rsqrt(jnp.mean(x * x, axis=-1, keepdims=True) + EPS)) * g


def _ada_kernel(c_ref, w_ref, b_ref, o_ref):
    c = c_ref[...]
    s = c * jax.nn.sigmoid(c)
    o_ref[...] = jnp.dot(s.astype(BF16), w_ref[...].astype(BF16),
                         preferred_element_type=F32) + b_ref[...]


def _modulation(cvec, w_ada, b_ada):
    tn = 1024
    return pl.pallas_call(
        _ada_kernel,
        grid=(N_MOD * D // tn,),
        in_specs=[pl.BlockSpec((MOD_ROWS, D), lambda j: (0, 0)),
                  pl.BlockSpec((D, tn), lambda j: (0, j)),
                  pl.BlockSpec((1, tn), lambda j: (0, j))],
        out_specs=pl.BlockSpec((MOD_ROWS, tn), lambda j: (0, j)),
        out_shape=jax.ShapeDtypeStruct((MOD_ROWS, N_MOD * D), F32),
        compiler_params=_cparams(("arbitrary",), 40),
        name="ada_modulation",
    )(cvec, w_ada, b_ada.reshape(1, -1))


NORM_TM = 512


def _ctx_lat_specs(tm, width):
    ctx_tiles = N_CTX // tm
    return (pl.BlockSpec((tm, width), lambda i, *_: (jnp.minimum(i, ctx_tiles - 1), 0)),
            pl.BlockSpec((tm, width), lambda i, *_: (jnp.maximum(i - ctx_tiles, 0), 0)))


def _norm_mod_kernel(xc_ref, xl_ref, g_ref, sh_ref, sc_ref, o_ref):
    x = jnp.where(pl.program_id(0) * NORM_TM < N_CTX, xc_ref[...], xl_ref[...])
    y = _rms(x, g_ref[...])
    o_ref[...] = (y * (1.0 + sc_ref[...]) + sh_ref[...]).astype(BF16)


def _norm_mod(x_ctx, x_lat, g, mod3):
    tm = NORM_TM
    return pl.pallas_call(
        _norm_mod_kernel,
        grid=(N_TOK // tm,),
        in_specs=[*_ctx_lat_specs(tm, D),
                  pl.BlockSpec((1, D), lambda i: (0, 0)),
                  _mod_spec(tm, MOD_SH1),
                  _mod_spec(tm, MOD_SC1)],
        out_specs=pl.BlockSpec((tm, D), lambda i: (i, 0)),
        out_shape=jax.ShapeDtypeStruct((N_TOK, D), BF16),
        compiler_params=_cparams(("arbitrary",), 40),
        name="norm_modulate",
    )(x_ctx, x_lat, g.reshape(1, D), mod3, mod3)


def _proj_kernel(h_ref, w_ref, o_ref, wb_ref):
    @pl.when(pl.program_id(1) == 0)
    def _():
        wb_ref[...] = w_ref[...].astype(BF16)

    o_ref[...] = jnp.dot(h_ref[...], wb_ref[...], preferred_element_type=F32).astype(BF16)


def _in_proj(h, w_in):
    tm, tn = 1024, 1024
    return pl.pallas_call(
        _proj_kernel,
        grid=(IN_COLS // tn, N_TOK // tm),
        in_specs=[pl.BlockSpec((tm, D), lambda n, m: (m, 0)),
                  pl.BlockSpec((D, tn), lambda n, m: (0, n))],
        out_specs=pl.BlockSpec((tm, tn), lambda n, m: (m, n)),
        out_shape=jax.ShapeDtypeStruct((N_TOK, IN_COLS), BF16),
        scratch_shapes=[pltpu.VMEM((D, tn), BF16)],
        compiler_params=_cparams(("arbitrary", "arbitrary"), 48),
        name="in_proj",
    )(h, w_in)


CONV_TM = 256


def _conv_kernel(cx_ref, cb_ref, cc_ref, mg0_ref, mg1_ref, cw_ref, cbias_ref, w_ref, o_ref):
    i = pl.program_id(0)
    u = cc_ref[...].astype(F32) * cx_ref[...].astype(F32)
    seg = jnp.where(i * CONV_TM < N_CTX, SEQ, GRID_W)
    pos = lax.broadcasted_iota(I32, (CONV_TM, 1), 0) & (seg - 1)
    prev = jnp.where(pos == 0, 0.0, pltpu.roll(u, 1, 0))
    nxt = jnp.where(pos == seg - 1, 0.0, pltpu.roll(u, CONV_TM - 1, 0))
    cw = cw_ref[...]
    uc = prev * cw[0:1, :] + u * cw[1:2, :] + nxt * cw[2:3, :] + cbias_ref[...]
    z = (cb_ref[...].astype(F32) * uc).astype(BF16)
    y = jnp.dot(z, w_ref[...], preferred_element_type=F32)
    mg = jnp.concatenate([mg0_ref[...], mg1_ref[...]], axis=1).astype(F32)
    o_ref[...] = (jax.nn.sigmoid(mg) * y).astype(BF16)


def _conv_branch(proj, conv_w, conv_b, w_br_conv_bf):
    tm = CONV_TM
    cw = 1024

    def col(c):
        return pl.BlockSpec((tm, cw), lambda i: (i, c))

    return pl.pallas_call(
        _conv_kernel,
        grid=(N_TOK // tm,),
        in_specs=[col(COL_CX // cw), col(COL_CB // cw), col(COL_CC // cw),
                  col(COL_MGC // cw), col(COL_MGC // cw + 1),
                  pl.BlockSpec((3, CONV_W), lambda i: (0, 0)),
                  pl.BlockSpec((1, CONV_W), lambda i: (0, 0)),
                  pl.BlockSpec((CONV_W, D), lambda i: (0, 0))],
        out_specs=pl.BlockSpec((tm, D), lambda i: (i, 0)),
        out_shape=jax.ShapeDtypeStruct((N_TOK, D), BF16),
        compiler_params=_cparams(("arbitrary",), 48),
        name="conv_branch",
    )(proj, proj, proj, proj, proj, conv_w, conv_b.reshape(1, CONV_W), w_br_conv_bf)


def _log_sigmoid(x):
    return -(jnp.maximum(-x, 0.0) + jnp.log1p(jnp.exp(-jnp.abs(x))))


def _rope(x, cos, sin_signed):
    lane = lax.broadcasted_iota(I32, x.shape, 1)
    swapped = jnp.where((lane & 63) < 32, pltpu.roll(x, 96, 1), pltpu.roll(x, 32, 1))
    return x * cos + swapped * sin_signed


def _retention_kernel(*refs, seq_len, latent):
    if latent:
        (dl_ref, q_ref, k_ref, v_ref, g_ref, gn_ref, cos_ref, sin_ref, s0_ref,
         r_ref, o_acc) = refs
    else:
        dl_ref, q_ref, k_ref, v_ref, g_ref, gn_ref, r_ref, sout_ref, o_acc = refs
    h = pl.program_id(1)
    c = CHUNK
    n_chunks = seq_len // c

    lgf = _log_sigmoid(jnp.full((c, c), dl_ref[0, h], F32))
    lgb = _log_sigmoid(jnp.full((c, c), dl_ref[1, h], F32))
    lgf_w = _log_sigmoid(jnp.full((c, DV), dl_ref[0, h], F32))
    lgb_w = _log_sigmoid(jnp.full((c, DV), dl_ref[1, h], F32))

    ii = lax.broadcasted_iota(I32, (c, c), 0).astype(F32)
    jj = lax.broadcasted_iota(I32, (c, c), 1).astype(F32)
    diff = ii - jj
    m_in = jnp.where(diff >= 0.0, jnp.exp(jnp.maximum(diff, 0.0) * lgf),
                     jnp.exp(jnp.maximum(-diff, 0.0) * lgb))
    io = lax.broadcasted_iota(I32, (c, DV), 0).astype(F32)
    dq_f = jnp.exp((io + 1.0) * lgf_w)
    dq_b = jnp.exp((c - io) * lgb_w)
    dk_f = jnp.exp((c - 1.0 - jj) * lgf)
    dk_b = jnp.exp(jj * lgb)
    dc_f = jnp.exp(c * lgf_w)
    dc_b = jnp.exp(c * lgb_w)

    q = q_ref[...].astype(F32)
    k = k_ref[...].astype(F32) * (DK ** -0.5)
    if latent:
        q = _rope(q, cos_ref[...], sin_ref[...])
        k = _rope(k, cos_ref[...], sin_ref[...])
        s_f = s0_ref[0]
        s_b = s0_ref[1]
    else:
        s_f = jnp.zeros((DK, DV), F32)
        s_b = jnp.zeros((DK, DV), F32)
    qb = q.astype(BF16)

    def chunk(x, n):
        return x[n * c:(n + 1) * c]

    for n in range(n_chunks):
        qn = chunk(qb, n)
        kt = chunk(k, n).T
        vn = v_ref[n * c:(n + 1) * c, :]
        s = jnp.dot(qn, kt.astype(BF16), preferred_element_type=F32)
        p = (s * m_in).astype(BF16)
        o = jnp.dot(p, vn, preferred_element_type=F32)
        o = o + jnp.dot(qn, s_f.astype(BF16), preferred_element_type=F32) * dq_f
        o_acc[n * c:(n + 1) * c, :] = o
        s_f = dc_f * s_f + jnp.dot((kt * dk_f).astype(BF16), vn, preferred_element_type=F32)

    for n in reversed(range(n_chunks)):
        qn = chunk(qb, n)
        kt = chunk(k, n).T
        vn = v_ref[n * c:(n + 1) * c, :]
        o_acc[n * c:(n + 1) * c, :] += (
            jnp.dot(qn, s_b.astype(BF16), preferred_element_type=F32) * dq_b)
        s_b = dc_b * s_b + jnp.dot((kt * dk_b).astype(BF16), vn, preferred_element_type=F32)

    if not latent:
        sout_ref[0] = s_f
        sout_ref[1] = s_b

    o = o_acc[...]
    mu = jnp.mean(o, axis=-1, keepdims=True)
    oc = o - mu
    var = jnp.mean(oc * oc, axis=-1, keepdims=True)
    on = oc * lax.rsqrt(var + EPS) * gn_ref[...]
    g = g_ref[...].astype(F32)
    r_ref[...] = ((g * jax.nn.sigmoid(g)) * on).astype(BF16)


def _retention(proj, decay_logit, gn_w, cos_t, sin_t, state0, *, latent):
    seq_len = DEC_SEQ if latent else SEQ
    nb = DEC_BATCH if latent else BATCH
    rb0 = (N_CTX // DEC_SEQ) if latent else 0

    def spec(width, col0):
        return pl.BlockSpec((seq_len, width), lambda b, h, dl: (rb0 + b, col0 // width + h))

    in_specs = [spec(DK, COL_Q), spec(DK, COL_K), spec(DV, COL_V), spec(DV, COL_G),
                pl.BlockSpec((1, DV), lambda b, h, dl: (0, h))]
    args = [proj, proj, proj, proj, gn_w.reshape(1, HEADS * DV)]
    state_spec = pl.BlockSpec((None, None, 2, None, DK, DV), lambda b, h, dl: (b, 0, 0, h, 0, 0))
    r_shape = jax.ShapeDtypeStruct((nb * seq_len, HEADS * DV), BF16)
    r_spec = pl.BlockSpec((seq_len, DV), lambda b, h, dl: (b, h))
    if latent:
        in_specs += [pl.BlockSpec((seq_len, DK), lambda b, h, dl: (0, 0)),
                     pl.BlockSpec((seq_len, DK), lambda b, h, dl: (0, 0)),
                     state_spec]
        args += [cos_t, sin_t, state0]
        out_specs, out_shape = r_spec, r_shape
    else:
        out_specs = (r_spec, state_spec)
        out_shape = (r_shape, jax.ShapeDtypeStruct((BATCH, 1, 2, HEADS, DK, DV), F32))
    return pl.pallas_call(
        functools.partial(_retention_kernel, seq_len=seq_len, latent=latent),
        grid_spec=pltpu.PrefetchScalarGridSpec(
            num_scalar_prefetch=1, grid=(nb, HEADS), in_specs=in_specs, out_specs=out_specs,
            scratch_shapes=[pltpu.VMEM((seq_len, DV), F32)]),
        out_shape=out_shape,
        compiler_params=_cparams(("arbitrary", "arbitrary"), 40),
        name="retention_latent" if latent else "retention_context",
    )(decay_logit, *args)


MIX_TM = 256


def _mix_kernel(rc_ref, rl_ref, wbr_ref, a_ref, mg0_ref, mg1_ref, wout_ref, xc_ref, xl_ref, g1_ref,
                sc2_ref, sh2_ref, npost_ref, npre_ref, rwt_ref, rb_ref,
                x1_ref, h2_ref, ti_ref, tw_ref):
    is_ctx = pl.program_id(0) * MIX_TM < N_CTX
    r = jnp.where(is_ctx, rc_ref[...], rl_ref[...])
    x = jnp.where(is_ctx, xc_ref[...], xl_ref[...])
    y = jnp.dot(r, wbr_ref[...], preferred_element_type=F32)
    mg = jnp.concatenate([mg0_ref[...], mg1_ref[...]], axis=1).astype(F32)
    mix = a_ref[...].astype(F32) + jax.nn.sigmoid(mg) * y
    out = jnp.dot(mix.astype(BF16), wout_ref[...], preferred_element_type=F32)
    x1 = x + g1_ref[...] * _rms(out, npost_ref[...])
    x1_ref[...] = x1
    h2 = _rms(x1, npre_ref[...]) * (1.0 + sc2_ref[...]) + sh2_ref[...]
    for s in range(LANE_CHUNKS):
        h2_ref[pl.ds(s, MIX_TM, stride=LANE_CHUNKS), :] = h2[:, s * 128:(s + 1) * 128]
    logits = lax.dot_general(rwt_ref[...], h2.astype(BF16), (((1,), (1,)), ((), ())),
                             preferred_element_type=F32) + rb_ref[...]
    eidx = lax.broadcasted_iota(I32, logits.shape, 0).astype(F32)
    vals, idxs = [], []
    for _ in range(TOP_K):
        m = jnp.max(logits, axis=0, keepdims=True)
        am = jnp.min(jnp.where(logits == m, eidx, float(N_EXP)), axis=0, keepdims=True)
        vals.append(m)
        idxs.append(am)
        logits = jnp.where(eidx == am, -jnp.inf, logits)
    es = [jnp.exp(v - vals[0]) for v in vals]
    denom = es[0] + es[1] + es[2] + es[3]
    for kk in range(TOP_K):
        ti_ref[kk:kk + 1, :] = idxs[kk].astype(I32)
        tw_ref[kk:kk + 1, :] = es[kk] / denom


def _mix_route(r_ctx, r_lat, wbr_bf, a, proj, wout_bf, x_ctx, x_lat, mod3, npost, npre, rwt_bf, rb):
    tm = MIX_TM
    cw = 1024

    def full(shape):
        return pl.BlockSpec(shape, lambda i: (0,) * len(shape), pipeline_mode=pl.Buffered(1))

    def row(width):
        return pl.BlockSpec((tm, width), lambda i: (i, 0))

    return pl.pallas_call(
        _mix_kernel,
        grid=(N_TOK // tm,),
        in_specs=[*_ctx_lat_specs(tm, D),
                  full((D, D)), row(D),
                  pl.BlockSpec((tm, cw), lambda i: (i, COL_MGR // cw)),
                  pl.BlockSpec((tm, cw), lambda i: (i, COL_MGR // cw + 1)),
                  full((D, D)), *_ctx_lat_specs(tm, D),
                  _mod_spec(tm, MOD_G1), _mod_spec(tm, MOD_SC2), _mod_spec(tm, MOD_SH2),
                  full((1, D)), full((1, D)), full((N_EXP, D)), full((N_EXP, 1))],
        out_specs=(row(D), pl.BlockSpec((tm * LANE_CHUNKS, 128), lambda i: (i, 0)),
                   pl.BlockSpec((TOP_K, tm), lambda i: (0, i)),
                   pl.BlockSpec((TOP_K, tm), lambda i: (0, i))),
        out_shape=(jax.ShapeDtypeStruct((N_TOK, D), F32),
                   jax.ShapeDtypeStruct((N_TOK * LANE_CHUNKS, 128), F32),
                   jax.ShapeDtypeStruct((TOP_K, N_TOK), I32),
                   jax.ShapeDtypeStruct((TOP_K, N_TOK), F32)),
        compiler_params=_cparams(("arbitrary",), 56),
        name="mix_out_route",
    )(r_ctx, r_lat, wbr_bf, a, proj, proj, wout_bf, x_ctx, x_lat, mod3, mod3, mod3,
      npost.reshape(1, D), npre.reshape(1, D), rwt_bf, rb.reshape(N_EXP, 1))


def _routing_tables(top_i):
    e = top_i.reshape(-1)
    onehot = e[:, None] == jnp.arange(N_EXP, dtype=I32)[None, :]
    blk = 256
    oh = onehot.astype(F32).reshape(N_PAIR // blk, blk, N_EXP)
    tri = (jnp.arange(blk)[:, None] >= jnp.arange(blk)[None, :]).astype(F32)
    within = jnp.einsum("ij,bjk->bik", tri, oh)
    blk_tot = within[:, -1, :]
    blk_off = jnp.cumsum(blk_tot, axis=0) - blk_tot
    csum = (within + blk_off[:, None, :]).reshape(N_PAIR, N_EXP)
    rank = jnp.sum(jnp.where(onehot, csum, 0.0), axis=1).astype(I32) - 1
    counts = csum[-1].astype(I32)
    tiles_e = (counts + MOE_TM - 1) // MOE_TM
    tile_end = jnp.cumsum(tiles_e)
    tile_start = tile_end - tiles_e
    slot = tile_start[e] * MOE_TM + rank
    n_tiles = tile_end[-1]
    tok = jnp.tile(jnp.arange(N_TOK, dtype=I32), TOP_K)
    tok_of_slot = jnp.zeros((MOE_ROWS,), I32).at[slot].set(tok, unique_indices=True)

    groups_e = (tiles_e + MOE_SUB - 1) // MOE_SUB
    g_end = jnp.cumsum(groups_e)
    g_start = g_end - groups_e
    n_groups = g_end[-1]
    gid = jnp.arange(MOE_GROUPS, dtype=I32)
    used = gid < n_groups
    g_exp = jnp.minimum(jnp.sum((gid[:, None] >= g_end[None, :]).astype(I32), axis=1), N_EXP - 1)
    g_exp = jnp.where(used, g_exp, g_exp[jnp.maximum(n_groups - 1, 0)])
    g_in_e = gid - g_start[g_exp]
    first = tile_start[g_exp] + MOE_SUB * g_in_e
    n_valid = jnp.where(used, jnp.clip(tiles_e[g_exp] - MOE_SUB * g_in_e, 0, MOE_SUB), 0)
    sub = jnp.arange(MOE_SUB, dtype=I32)[None, :]
    x_tile = first[:, None] + jnp.minimum(sub, jnp.maximum(n_valid[:, None] - 1, 0))
    x_tile = jnp.where(used[:, None], x_tile, x_tile[jnp.maximum(n_groups - 1, 0)][None, :])
    tail = n_tiles + (gid[:, None] - n_groups) * MOE_SUB + sub
    o_tile = jnp.where(used[:, None],
                       jnp.where(sub < n_valid[:, None], first[:, None] + sub, -1),
                       jnp.where(tail < MOE_TILES, tail, -1))
    return (slot, tok_of_slot, n_tiles.reshape(1).astype(I32), x_tile.reshape(-1).astype(I32),
            o_tile.reshape(-1).astype(I32), n_valid.astype(I32), g_exp.astype(I32))


def _row_copy(src_hbm, row, dst_ref, r, sem):
    return pltpu.make_async_copy(src_hbm.at[pl.ds(row, 1), :], dst_ref.at[pl.ds(r, 1), :], sem)


def _chunked_row_copy(src_hbm, row, dst_ref, r, sem):
    return pltpu.make_async_copy(
        src_hbm.at[pl.ds(pl.multiple_of(row * LANE_CHUNKS, LANE_CHUNKS), LANE_CHUNKS), :],
        dst_ref.at[pl.ds(pl.multiple_of(r * LANE_CHUNKS, LANE_CHUNKS), LANE_CHUNKS), :], sem)


DISP_RING = 4
DISP_LAG = DISP_RING - 1


def _dispatch_kernel(nt_ref, tos_ref, x_hbm, o_ref, buf, sem):
    j = pl.program_id(0)
    nt = nt_ref[0]
    done = j - DISP_LAG

    @pl.when(j < nt)
    def _():
        b = j & (DISP_RING - 1)

        def issue(r, carry):
            _chunked_row_copy(x_hbm, tos_ref[0, r], buf.at[b], r, sem.at[b]).start()
            return carry

        lax.fori_loop(0, MOE_TM, issue, 0, unroll=8)

    @pl.when(jnp.logical_and(done >= 0, done < nt))
    def _():
        b = done & (DISP_RING - 1)

        def wait(r, carry):
            _chunked_row_copy(x_hbm, 0, buf.at[b], r, sem.at[b]).wait()
            return carry

        lax.fori_loop(0, MOE_TM, wait, 0, unroll=8)
        for s in range(LANE_CHUNKS):
            o_ref[:, s * 128:(s + 1) * 128] = (
                buf.at[b][pl.ds(s, MOE_TM, stride=LANE_CHUNKS), :].astype(BF16))

    @pl.when(jnp.logical_and(done >= 0, done >= nt))
    def _():
        o_ref[...] = jnp.zeros_like(o_ref)


def _dispatch(h2, tok_of_slot, n_tiles):
    return pl.pallas_call(
        _dispatch_kernel,
        grid_spec=pltpu.PrefetchScalarGridSpec(
            num_scalar_prefetch=1, grid=(MOE_TILES + DISP_LAG,),
            in_specs=[pl.BlockSpec((None, 1, MOE_TM),
                                   lambda j, nt: (jnp.minimum(j, MOE_TILES - 1), 0, 0),
                                   memory_space=pltpu.SMEM),
                      pl.BlockSpec(memory_space=pl.ANY)],
            out_specs=pl.BlockSpec((MOE_TM, D), lambda j, nt: (jnp.maximum(j - DISP_LAG, 0), 0)),
            scratch_shapes=[pltpu.VMEM((DISP_RING, MOE_TM * LANE_CHUNKS, 128), F32),
                            pltpu.SemaphoreType.DMA((DISP_RING,))]),
        out_shape=jax.ShapeDtypeStruct((MOE_ROWS, D), BF16),
        compiler_params=_cparams(("arbitrary",), 32),
        name="moe_dispatch",
    )(n_tiles, tok_of_slot.reshape(MOE_TILES, 1, MOE_TM), h2)


def _tile_store(acc_ref, b, i, y_hbm, tile, sem):
    return pltpu.make_async_copy(acc_ref.at[b, i], y_hbm.at[pl.ds(tile * MOE_TM, MOE_TM), :],
                                 sem.at[b])


def _moe_kernel(xt_ref, ot_ref, nv_ref, ge_ref, *refs):
    x_refs = refs[:MOE_SUB]
    wg_ref, wu_ref, wd_ref, bg_ref, bu_ref, bd_ref, y_hbm = refs[MOE_SUB:MOE_SUB + 7]
    wgb_ref, wub_ref, wdb_ref, acc_ref, sem = refs[MOE_SUB + 7:]
    g = pl.program_id(0)
    c = pl.program_id(1)
    nv = nv_ref[g]
    b = g & 1
    last_chunk = c == MOE_NCH - 1

    def wait_group(gw):
        for i in range(MOE_SUB):
            @pl.when(ot_ref[gw * MOE_SUB + i] >= 0)
            def _(i=i):
                _tile_store(acc_ref, gw & 1, i, y_hbm, 0, sem).wait()

    @pl.when(jnp.logical_and(c == 0, g >= 2))
    def _():
        wait_group(g - 2)

    @pl.when(jnp.logical_and(c == 0, nv > 0))
    def _():
        for i in range(MOE_SUB):
            acc_ref[b, i] = jnp.broadcast_to(bd_ref[...], (MOE_TM, D))

    def group_step(m):
        wgb_ref[...] = wg_ref[...].astype(BF16)
        wub_ref[...] = wu_ref[...].astype(BF16)
        wdb_ref[...] = wd_ref[...].astype(BF16)
        for i in range(m):
            x = x_refs[i][...]
            gate = jnp.dot(x, wgb_ref[...], preferred_element_type=F32) + bg_ref[...]
            up = jnp.dot(x, wub_ref[...], preferred_element_type=F32) + bu_ref[...]
            gate = jnp.minimum(gate, LIMIT)
            up = jnp.clip(up, -LIMIT, LIMIT)
            act = (gate * jax.nn.sigmoid(ALPHA * gate) * (up + 1.0)).astype(BF16)
            acc_ref[b, i] += jnp.dot(act, wdb_ref[...], preferred_element_type=F32)

    for m in range(1, MOE_SUB + 1):
        pl.when(nv == m)(functools.partial(group_step, m))

    for i in range(MOE_SUB):
        @pl.when(jnp.logical_and(jnp.logical_and(i >= nv, last_chunk), ot_ref[g * MOE_SUB + i] >= 0))
        def _(i=i):
            acc_ref[b, i] = jnp.zeros((MOE_TM, D), F32)

        @pl.when(jnp.logical_and(last_chunk, ot_ref[g * MOE_SUB + i] >= 0))
        def _(i=i):
            _tile_store(acc_ref, b, i, y_hbm, ot_ref[g * MOE_SUB + i], sem).start()

    @pl.when(jnp.logical_and(last_chunk, g == MOE_GROUPS - 1))
    def _():
        wait_group(g - 1)
        wait_group(g)


def _moe(xs, wg, wu, wd, bg, bu, bd, x_tile, o_tile, n_valid, g_exp):
    def chunk_of(g, c, nv):
        return jnp.where(nv[g] > 0, c, MOE_NCH - 1)

    def x_spec(i):
        return pl.BlockSpec((MOE_TM, D), lambda g, c, xt, ot, nv, ge: (xt[g * MOE_SUB + i], 0))

    w_in_spec = pl.BlockSpec((None, D, MOE_CH),
                             lambda g, c, xt, ot, nv, ge: (ge[g], 0, chunk_of(g, c, nv)))
    b_in_spec = pl.BlockSpec((None, 1, MOE_CH),
                             lambda g, c, xt, ot, nv, ge: (ge[g], 0, chunk_of(g, c, nv)))
    w_out_spec = pl.BlockSpec((None, MOE_CH, D),
                              lambda g, c, xt, ot, nv, ge: (ge[g], chunk_of(g, c, nv), 0))
    b_out_spec = pl.BlockSpec((None, 1, D), lambda g, c, xt, ot, nv, ge: (ge[g], 0, 0))
    return pl.pallas_call(
        _moe_kernel,
        grid_spec=pltpu.PrefetchScalarGridSpec(
            num_scalar_prefetch=4, grid=(MOE_GROUPS, MOE_NCH),
            in_specs=[*[x_spec(i) for i in range(MOE_SUB)],
                      w_in_spec, w_in_spec, w_out_spec, b_in_spec, b_in_spec, b_out_spec],
            out_specs=pl.BlockSpec(memory_space=pl.ANY),
            scratch_shapes=[pltpu.VMEM((D, MOE_CH), BF16), pltpu.VMEM((D, MOE_CH), BF16),
                            pltpu.VMEM((MOE_CH, D), BF16),
                            pltpu.VMEM((2, MOE_SUB, MOE_TM, D), F32),
                            pltpu.SemaphoreType.DMA((2,))]),
        out_shape=jax.ShapeDtypeStruct((MOE_ROWS, D), F32),
        compiler_params=_cparams(("arbitrary", "arbitrary"), 62),
        name="moe_experts",
    )(x_tile, o_tile, n_valid, g_exp, *([xs] * MOE_SUB), wg, wu, wd,
      bg.reshape(N_EXP, 1, D_FF), bu.reshape(N_EXP, 1, D_FF), bd.reshape(N_EXP, 1, D))


COMB_TM = 256


COMB_TILES = N_TOK // COMB_TM


def _combine_kernel(slot_ref, ys_hbm, w_ref, x1_ref, g2_ref, npost_ref, oc_ref, ol_ref, buf, sem):
    i = pl.program_id(0)

    @pl.when(i < COMB_TILES)
    def _():
        b = i & 1
        for kk in range(TOP_K):
            def issue(r, carry, kk=kk):
                _row_copy(ys_hbm, slot_ref[kk, r], buf.at[b, kk], r, sem.at[b]).start()
                return carry

            lax.fori_loop(0, COMB_TM, issue, 0, unroll=8)

    @pl.when(i >= 1)
    def _():
        b = (i - 1) & 1
        for kk in range(TOP_K):
            def wait(r, carry, kk=kk):
                _row_copy(ys_hbm, 0, buf.at[b, kk], r, sem.at[b]).wait()
                return carry

            lax.fori_loop(0, COMB_TM, wait, 0, unroll=8)
        w = w_ref[...]
        f = buf[b, 0] * w[:, 0:1]
        for kk in range(1, TOP_K):
            f = f + buf[b, kk] * w[:, kk:kk + 1]
        out = x1_ref[...] + g2_ref[...] * _rms(f, npost_ref[...])
        base = (i - 1) * COMB_TM

        @pl.when(base < N_CTX)
        def _():
            oc_ref[...] = out

        @pl.when(base >= N_CTX)
        def _():
            ol_ref[...] = out


def _combine(ys, slot, top_w_t, x1, mod3, npost):
    tm = COMB_TM

    def prev(i):
        return jnp.maximum(i - 1, 0)

    oc_spec, ol_spec = _ctx_lat_specs(tm, D)
    return pl.pallas_call(
        _combine_kernel,
        grid=(COMB_TILES + 1,),
        in_specs=[pl.BlockSpec((TOP_K, tm), lambda i: (0, jnp.minimum(i, COMB_TILES - 1)),
                               memory_space=pltpu.SMEM),
                  pl.BlockSpec(memory_space=pl.ANY),
                  pl.BlockSpec((tm, TOP_K), lambda i: (prev(i), 0)),
                  pl.BlockSpec((tm, D), lambda i: (prev(i), 0)),
                  pl.BlockSpec((None, 1, D), lambda i: (_mod_row(prev(i) * tm), 0, MOD_G2)),
                  pl.BlockSpec((1, D), lambda i: (0, 0))],
        out_specs=(pl.BlockSpec((tm, D), lambda i: oc_spec.index_map(prev(i))),
                   pl.BlockSpec((tm, D), lambda i: ol_spec.index_map(prev(i)))),
        scratch_shapes=[pltpu.VMEM((2, TOP_K, tm, D), F32), pltpu.SemaphoreType.DMA((2,))],
        out_shape=(jax.ShapeDtypeStruct((N_CTX, D), F32), jax.ShapeDtypeStruct((N_LAT, D), F32)),
        compiler_params=_cparams(("arbitrary",), 48),
        name="moe_combine",
    )(slot.reshape(TOP_K, N_TOK), ys, top_w_t, x1, mod3, npost.reshape(1, D))


def _rope_tables():
    pos = jnp.arange(DEC_SEQ)
    row = (pos // GRID_W).astype(F32)
    col = (pos % GRID_W).astype(F32)
    n_freq = DK // 4
    freqs = THETA ** (-jnp.arange(n_freq, dtype=F32) / n_freq)
    ar = row[:, None] * freqs
    ac = col[:, None] * freqs
    cos_t = jnp.concatenate([jnp.cos(ar), jnp.cos(ar), jnp.cos(ac), jnp.cos(ac)], axis=1)
    sin_t = jnp.concatenate([-jnp.sin(ar), jnp.sin(ar), -jnp.sin(ac), jnp.sin(ac)], axis=1)
    return cos_t.astype(F32), sin_t.astype(F32)


def kernel(x_prompt, x_sample, state_ret, c, c_ctx, w_ada, b_ada, norm_mix_pre, norm_mix_post,
           norm_ffn_pre, norm_ffn_post, w_in, conv_w, conv_b, ret_decay_logit, ret_gn_w,
           w_br_conv, w_br_ret, w_out, router_w, router_b, moe_w_gate, moe_b_gate,
           moe_w_up, moe_b_up, moe_w_down, moe_b_down):
    layer = 0
    x_ctx = x_prompt.reshape(N_CTX, D)
    x_lat = x_sample.reshape(N_LAT, D)
    cvec = jnp.concatenate(
        [c_ctx[None, :], c, jnp.zeros((MOD_ROWS - 1 - DEC_BATCH, D), F32)], axis=0)

    mod = _modulation(cvec, w_ada[layer], b_ada[layer])
    mod3 = mod.reshape(MOD_ROWS, 1, N_MOD * D)

    h1 = _norm_mod(x_ctx, x_lat, norm_mix_pre[layer], mod3)
    proj = _in_proj(h1, w_in[layer])

    a = _conv_branch(proj, conv_w[layer], conv_b[layer], w_br_conv[layer].astype(BF16))

    cos_t, sin_t = _rope_tables()
    dl = ret_decay_logit[layer].astype(F32)
    r_ctx, new_state = _retention(proj, dl, ret_gn_w[layer], None, None, None, latent=False)
    r_lat = _retention(proj, dl, ret_gn_w[layer], cos_t, sin_t, state_ret, latent=True)

    x1, h2, top_i, top_w = _mix_route(
        r_ctx, r_lat, w_br_ret[layer].astype(BF16), a, proj, w_out[layer].astype(BF16),
        x_ctx, x_lat, mod3, norm_mix_post[layer], norm_ffn_pre[layer],
        router_w[layer].T.astype(BF16), router_b[layer])

    slot, tok_of_slot, n_tiles, x_tile, o_tile, n_valid, g_exp = _routing_tables(top_i)
    xs = _dispatch(h2, tok_of_slot, n_tiles)
    ys = _moe(xs, moe_w_gate[layer], moe_w_up[layer], moe_w_down[layer],
              moe_b_gate[layer], moe_b_up[layer], moe_b_down[layer],
              x_tile, o_tile, n_valid, g_exp)
    y_ctx, y_lat = _combine(ys, slot, top_w.T, x1, mod3, norm_ffn_post[layer])

    return (y_ctx.reshape(BATCH, SEQ, D), y_lat.reshape(DEC_BATCH, DEC_SEQ, D), new_state)
```

```python
import functools

import jax
import jax.numpy as jnp
import numpy as np
from jax import lax
from jax.experimental import pallas as pl
from jax.experimental.pallas import tpu as pltpu

F32 = jnp.float32
BF16 = jnp.bfloat16
I32 = jnp.int32

D = 2048
BATCH = 16
SEQ = 256
DEC_BATCH = 2
DEC_SEQ = 1024
GRID_W = 64
CONV_W = 1024
HEADS = 8
DK = 128
DV = 256
CHUNK = 128
N_EXP = 32
TOP_K = 4
D_FF = 2048
LIMIT = 7.0
ALPHA = 1.702
THETA = 10000.0
EPS = 1e-6
N_MOD = 6
N_CTX = BATCH * SEQ
N_LAT = DEC_BATCH * DEC_SEQ
N_TOK = N_CTX + N_LAT
IN_COLS = 3 * CONV_W + 2 * HEADS * DK + 2 * HEADS * DV + 2 * D
COL_CX, COL_CB, COL_CC = 0, CONV_W, 2 * CONV_W
COL_Q = 3 * CONV_W
COL_K = COL_Q + HEADS * DK
COL_V = COL_K + HEADS * DK
COL_G = COL_V + HEADS * DV
COL_MGC = COL_G + HEADS * DV
COL_MGR = COL_MGC + D
MOD_SH1, MOD_SC1, MOD_G1, MOD_SH2, MOD_SC2, MOD_G2 = range(6)
MOD_ROWS = 8

MOE_TM = 256
N_PAIR = N_TOK * TOP_K
MOE_TILES = (N_PAIR + N_EXP * (MOE_TM - 1)) // MOE_TM + 1
MOE_ROWS = MOE_TILES * MOE_TM
MOE_SUB = 4
MOE_GROUPS = (MOE_TILES + (MOE_SUB - 1) * N_EXP) // MOE_SUB
MOE_CH = 512
MOE_NCH = D_FF // MOE_CH

MIB = 1024 * 1024


def _cparams(sem, vmem_mib):
    return pltpu.CompilerParams(dimension_semantics=sem, vmem_limit_bytes=vmem_mib * MIB)


def _mod_row(t0):
    return jnp.where(t0 < N_CTX, 0, 1 + (t0 - N_CTX) // DEC_SEQ)


def _mod_spec(tm, chunk):
    return pl.BlockSpec((None, 1, D), lambda i: (_mod_row(i * tm), 0, chunk))


def _rms(x, g):
    return (x * lax.rsqrt(jnp.mean(x * x, axis=-1, keepdims=True) + EPS)) * g


def _ada_kernel(c_ref, w_ref, b_ref, o_ref):
    c = c_ref[...]
    s = c * jax.nn.sigmoid(c)
    o_ref[...] = jnp.dot(s.astype(BF16), w_ref[...].astype(BF16),
                         preferred_element_type=F32) + b_ref[...]


def _modulation(cvec, w_ada, b_ada):
    tn = 1024
    return pl.pallas_call(
        _ada_kernel,
        grid=(N_MOD * D // tn,),
        in_specs=[pl.BlockSpec((MOD_ROWS, D), lambda j: (0, 0)),
                  pl.BlockSpec((D, tn), lambda j: (0, j)),
                  pl.BlockSpec((1, tn), lambda j: (0, j))],
        out_specs=pl.BlockSpec((MOD_ROWS, tn), lambda j: (0, j)),
        out_shape=jax.ShapeDtypeStruct((MOD_ROWS, N_MOD * D), F32),
        compiler_params=_cparams(("arbitrary",), 40),
        name="ada_modulation",
    )(cvec, w_ada, b_ada.reshape(1, -1))


NORM_TM = 512


def _ctx_lat_specs(tm, width):
    ctx_tiles = N_CTX // tm
    return (pl.BlockSpec((tm, width), lambda i, *_: (jnp.minimum(i, ctx_tiles - 1), 0)),
            pl.BlockSpec((tm, width), lambda i, *_: (jnp.maximum(i - ctx_tiles, 0), 0)))


def _norm_mod_kernel(xc_ref, xl_ref, g_ref, sh_ref, sc_ref, o_ref):
    x = jnp.where(pl.program_id(0) * NORM_TM < N_CTX, xc_ref[...], xl_ref[...])
    y = _rms(x, g_ref[...])
    o_ref[...] = (y * (1.0 + sc_ref[...]) + sh_ref[...]).astype(BF16)


def _norm_mod(x_ctx, x_lat, g, mod3):
    tm = NORM_TM
    return pl.pallas_call(
        _norm_mod_kernel,
        grid=(N_TOK // tm,),
        in_specs=[*_ctx_lat_specs(tm, D),
                  pl.BlockSpec((1, D), lambda i: (0, 0)),
                  _mod_spec(tm, MOD_SH1),
                  _mod_spec(tm, MOD_SC1)],
        out_specs=pl.BlockSpec((tm, D), lambda i: (i, 0)),
        out_shape=jax.ShapeDtypeStruct((N_TOK, D), BF16),
        compiler_params=_cparams(("arbitrary",), 40),
        name="norm_modulate",
    )(x_ctx, x_lat, g.reshape(1, D), mod3, mod3)


def _proj_kernel(h_ref, w_ref, o_ref, wb_ref):
    @pl.when(pl.program_id(1) == 0)
    def _():
        wb_ref[...] = w_ref[...].astype(BF16)

    o_ref[...] = jnp.dot(h_ref[...], wb_ref[...], preferred_element_type=F32).astype(BF16)


def _in_proj(h, w_in):
    tm, tn = 1024, 1024
    return pl.pallas_call(
        _proj_kernel,
        grid=(IN_COLS // tn, N_TOK // tm),
        in_specs=[pl.BlockSpec((tm, D), lambda n, m: (m, 0)),
                  pl.BlockSpec((D, tn), lambda n, m: (0, n))],
        out_specs=pl.BlockSpec((tm, tn), lambda n, m: (m, n)),
        out_shape=jax.ShapeDtypeStruct((N_TOK, IN_COLS), BF16),
        scratch_shapes=[pltpu.VMEM((D, tn), BF16)],
        compiler_params=_cparams(("arbitrary", "arbitrary"), 48),
        name="in_proj",
    )(h, w_in)


CONV_TM = 256


def _conv_kernel(cx_ref, cb_ref, cc_ref, mg0_ref, mg1_ref, cw_ref, cbias_ref, w_ref, o_ref):
    i = pl.program_id(0)
    u = cc_ref[...].astype(F32) * cx_ref[...].astype(F32)
    seg = jnp.where(i * CONV_TM < N_CTX, SEQ, GRID_W)
    pos = lax.broadcasted_iota(I32, (CONV_TM, 1), 0) & (seg - 1)
    prev = jnp.where(pos == 0, 0.0, pltpu.roll(u, 1, 0))
    nxt = jnp.where(pos == seg - 1, 0.0, pltpu.roll(u, CONV_TM - 1, 0))
    cw = cw_ref[...]
    uc = prev * cw[0:1, :] + u * cw[1:2, :] + nxt * cw[2:3, :] + cbias_ref[...]
    z = (cb_ref[...].astype(F32) * uc).astype(BF16)
    y = jnp.dot(z, w_ref[...], preferred_element_type=F32)
    mg = jnp.concatenate([mg0_ref[...], mg1_ref[...]], axis=1).astype(F32)
    o_ref[...] = (jax.nn.sigmoid(mg) * y).astype(BF16)


def _conv_branch(proj, conv_w, conv_b, w_br_conv_bf):
    tm = CONV_TM
    cw = 1024

    def col(c):
        return pl.BlockSpec((tm, cw), lambda i: (i, c))

    return pl.pallas_call(
        _conv_kernel,
        grid=(N_TOK // tm,),
        in_specs=[col(COL_CX // cw), col(COL_CB // cw), col(COL_CC // cw),
                  col(COL_MGC // cw), col(COL_MGC // cw + 1),
                  pl.BlockSpec((3, CONV_W), lambda i: (0, 0)),
                  pl.BlockSpec((1, CONV_W), lambda i: (0, 0)),
                  pl.BlockSpec((CONV_W, D), lambda i: (0, 0))],
        out_specs=pl.BlockSpec((tm, D), lambda i: (i, 0)),
        out_shape=jax.ShapeDtypeStruct((N_TOK, D), BF16),
        compiler_params=_cparams(("arbitrary",), 48),
        name="conv_branch",
    )(proj, proj, proj, proj, proj, conv_w, conv_b.reshape(1, CONV_W), w_br_conv_bf)


def _log_sigmoid(x):
    return -(jnp.maximum(-x, 0.0) + jnp.log1p(jnp.exp(-jnp.abs(x))))


def _rope(x, cos, sin_signed):
    lane = lax.broadcasted_iota(I32, x.shape, 1)
    swapped = jnp.where((lane & 63) < 32, pltpu.roll(x, 96, 1), pltpu.roll(x, 32, 1))
    return x * cos + swapped * sin_signed


def _retention_kernel(*refs, seq_len, latent, hps):
    for hh in range(hps):
        _retention_head(refs, pl.program_id(1) * hps + hh, hh, seq_len=seq_len, latent=latent)


def _retention_head(refs, h, hh, *, seq_len, latent):
    if latent:
        (dl_ref, q_ref, k_ref, v_ref, g_ref, gn_ref, cos_ref, sin_ref, s0_ref,
         r_ref, o_acc) = refs
    else:
        dl_ref, q_ref, k_ref, v_ref, g_ref, gn_ref, r_ref, sout_ref, o_acc = refs
    c = CHUNK
    n_chunks = seq_len // c
    kcol = slice(hh * DK, (hh + 1) * DK)
    vcol = slice(hh * DV, (hh + 1) * DV)

    lgf = _log_sigmoid(jnp.full((c, c), dl_ref[0, h], F32))
    lgb = _log_sigmoid(jnp.full((c, c), dl_ref[1, h], F32))
    lgf_w = _log_sigmoid(jnp.full((c, DV), dl_ref[0, h], F32))
    lgb_w = _log_sigmoid(jnp.full((c, DV), dl_ref[1, h], F32))

    ii = lax.broadcasted_iota(I32, (c, c), 0).astype(F32)
    jj = lax.broadcasted_iota(I32, (c, c), 1).astype(F32)
    diff = ii - jj
    m_in = jnp.where(diff >= 0.0, jnp.exp(jnp.maximum(diff, 0.0) * lgf),
                     jnp.exp(jnp.maximum(-diff, 0.0) * lgb))
    io = lax.broadcasted_iota(I32, (c, DV), 0).astype(F32)
    dq_f = jnp.exp((io + 1.0) * lgf_w)
    dq_b = jnp.exp((c - io) * lgb_w)
    dk_f = jnp.exp((c - 1.0 - jj) * lgf)
    dk_b = jnp.exp(jj * lgb)
    dc_f = jnp.exp(c * lgf_w)
    dc_b = jnp.exp(c * lgb_w)

    q = q_ref[:, kcol].astype(F32)
    k = k_ref[:, kcol].astype(F32) * (DK ** -0.5)
    if latent:
        q = _rope(q, cos_ref[...], sin_ref[...])
        k = _rope(k, cos_ref[...], sin_ref[...])
        s_f = s0_ref[0, hh]
        s_b = s0_ref[1, hh]
    else:
        s_f = jnp.zeros((DK, DV), F32)
        s_b = jnp.zeros((DK, DV), F32)
    qb = q.astype(BF16)

    def chunk(x, n):
        return x[n * c:(n + 1) * c]

    for n in range(n_chunks):
        qn = chunk(qb, n)
        kt = chunk(k, n).T
        vn = v_ref[n * c:(n + 1) * c, vcol]
        s = jnp.dot(qn, kt.astype(BF16), preferred_element_type=F32)
        p = (s * m_in).astype(BF16)
        o = jnp.dot(p, vn, preferred_element_type=F32)
        o = o + jnp.dot(qn, s_f.astype(BF16), preferred_element_type=F32) * dq_f
        o_acc[n * c:(n + 1) * c, vcol] = o
        s_f = dc_f * s_f + jnp.dot((kt * dk_f).astype(BF16), vn, preferred_element_type=F32)

    for n in reversed(range(n_chunks)):
        qn = chunk(qb, n)
        kt = chunk(k, n).T
        vn = v_ref[n * c:(n + 1) * c, vcol]
        o_acc[n * c:(n + 1) * c, vcol] += (
            jnp.dot(qn, s_b.astype(BF16), preferred_element_type=F32) * dq_b)
        s_b = dc_b * s_b + jnp.dot((kt * dk_b).astype(BF16), vn, preferred_element_type=F32)

    if not latent:
        sout_ref[0, hh] = s_f
        sout_ref[1, hh] = s_b

    o = o_acc[:, vcol]
    mu = jnp.mean(o, axis=-1, keepdims=True)
    oc = o - mu
    var = jnp.mean(oc * oc, axis=-1, keepdims=True)
    on = oc * lax.rsqrt(var + EPS) * gn_ref[:, vcol]
    g = g_ref[:, vcol].astype(F32)
    r_ref[:, vcol] = ((g * jax.nn.sigmoid(g)) * on).astype(BF16)


def _retention(proj, decay_logit, gn_w, cos_t, sin_t, state0, *, latent):
    seq_len = DEC_SEQ if latent else SEQ
    nb = DEC_BATCH if latent else BATCH
    rb0 = (N_CTX // DEC_SEQ) if latent else 0

    hps = 1 if latent else 4
    assert COL_V % (DV * hps) == 0 and COL_G % (DV * hps) == 0 and COL_Q % (DK * hps) == 0

    def spec(width, col0):
        w = width * hps
        return pl.BlockSpec((seq_len, w), lambda b, h, dl: (rb0 + b, col0 // w + h))

    in_specs = [spec(DK, COL_Q), spec(DK, COL_K), spec(DV, COL_V), spec(DV, COL_G),
                pl.BlockSpec((1, DV * hps), lambda b, h, dl: (0, h))]
    args = [proj, proj, proj, proj, gn_w.reshape(1, HEADS * DV)]
    state_spec = pl.BlockSpec((None, None, 2, hps, DK, DV), lambda b, h, dl: (b, 0, 0, h, 0, 0))
    r_shape = jax.ShapeDtypeStruct((nb * seq_len, HEADS * DV), BF16)
    r_spec = pl.BlockSpec((seq_len, DV * hps), lambda b, h, dl: (b, h))
    if latent:
        in_specs += [pl.BlockSpec((seq_len, DK), lambda b, h, dl: (0, 0)),
                     pl.BlockSpec((seq_len, DK), lambda b, h, dl: (0, 0)),
                     state_spec]
        args += [cos_t, sin_t, state0]
        out_specs, out_shape = r_spec, r_shape
    else:
        out_specs = (r_spec, state_spec)
        out_shape = (r_shape, jax.ShapeDtypeStruct((BATCH, 1, 2, HEADS, DK, DV), F32))
    return pl.pallas_call(
        functools.partial(_retention_kernel, seq_len=seq_len, latent=latent, hps=hps),
        grid_spec=pltpu.PrefetchScalarGridSpec(
            num_scalar_prefetch=1, grid=(nb, HEADS // hps), in_specs=in_specs,
            out_specs=out_specs, scratch_shapes=[pltpu.VMEM((seq_len, DV * hps), F32)]),
        out_shape=out_shape,
        compiler_params=_cparams(("arbitrary", "arbitrary"), 40),
        name="retention_latent" if latent else "retention_context",
    )(decay_logit, *args)


MIX_TM = 256


def _mix_kernel(rc_ref, rl_ref, wbr_ref, a_ref, mg0_ref, mg1_ref, wout_ref, xc_ref, xl_ref, g1_ref,
                sc2_ref, sh2_ref, npost_ref, npre_ref, rwt_ref, rb_ref,
                x1_ref, h2_ref, ti_ref, tw_ref):
    is_ctx = pl.program_id(0) * MIX_TM < N_CTX
    r = jnp.where(is_ctx, rc_ref[...], rl_ref[...])
    x = jnp.where(is_ctx, xc_ref[...], xl_ref[...])
    y = jnp.dot(r, wbr_ref[...], preferred_element_type=F32)
    mg = jnp.concatenate([mg0_ref[...], mg1_ref[...]], axis=1).astype(F32)
    mix = a_ref[...].astype(F32) + jax.nn.sigmoid(mg) * y
    out = jnp.dot(mix.astype(BF16), wout_ref[...], preferred_element_type=F32)
    x1 = x + g1_ref[...] * _rms(out, npost_ref[...])
    x1_ref[...] = x1
    h2 = _rms(x1, npre_ref[...]) * (1.0 + sc2_ref[...]) + sh2_ref[...]
    h2_ref[...] = h2
    logits = lax.dot_general(rwt_ref[...], h2.astype(BF16), (((1,), (1,)), ((), ())),
                             preferred_element_type=F32) + rb_ref[...]
    eidx = lax.broadcasted_iota(I32, logits.shape, 0).astype(F32)
    vals, idxs = [], []
    for _ in range(TOP_K):
        m = jnp.max(logits, axis=0, keepdims=True)
        am = jnp.min(jnp.where(logits == m, eidx, float(N_EXP)), axis=0, keepdims=True)
        vals.append(m)
        idxs.append(am)
        logits = jnp.where(eidx == am, -jnp.inf, logits)
    es = [jnp.exp(v - vals[0]) for v in vals]
    denom = es[0] + es[1] + es[2] + es[3]
    for kk in range(TOP_K):
        ti_ref[kk:kk + 1, :] = idxs[kk].astype(I32)
        tw_ref[kk:kk + 1, :] = es[kk] / denom


def _mix_route(r_ctx, r_lat, wbr_bf, a, proj, wout_bf, x_ctx, x_lat, mod3, npost, npre, rwt_bf, rb):
    tm = MIX_TM
    cw = 1024

    def full(shape):
        return pl.BlockSpec(shape, lambda i: (0,) * len(shape), pipeline_mode=pl.Buffered(1))

    def row(width):
        return pl.BlockSpec((tm, width), lambda i: (i, 0))

    return pl.pallas_call(
        _mix_kernel,
        grid=(N_TOK // tm,),
        in_specs=[*_ctx_lat_specs(tm, D),
                  full((D, D)), row(D),
                  pl.BlockSpec((tm, cw), lambda i: (i, COL_MGR // cw)),
                  pl.BlockSpec((tm, cw), lambda i: (i, COL_MGR // cw + 1)),
                  full((D, D)), *_ctx_lat_specs(tm, D),
                  _mod_spec(tm, MOD_G1), _mod_spec(tm, MOD_SC2), _mod_spec(tm, MOD_SH2),
                  full((1, D)), full((1, D)), full((N_EXP, D)), full((N_EXP, 1))],
        out_specs=(row(D), row(D),
                   pl.BlockSpec((TOP_K, tm), lambda i: (0, i)),
                   pl.BlockSpec((TOP_K, tm), lambda i: (0, i))),
        out_shape=(jax.ShapeDtypeStruct((N_TOK, D), F32),
                   jax.ShapeDtypeStruct((N_TOK, D), F32),
                   jax.ShapeDtypeStruct((TOP_K, N_TOK), I32),
                   jax.ShapeDtypeStruct((TOP_K, N_TOK), F32)),
        compiler_params=_cparams(("arbitrary",), 56),
        name="mix_out_route",
    )(r_ctx, r_lat, wbr_bf, a, proj, proj, wout_bf, x_ctx, x_lat, mod3, mod3, mod3,
      npost.reshape(1, D), npre.reshape(1, D), rwt_bf, rb.reshape(N_EXP, 1))


def _routing_tables(top_i):
    e = top_i.reshape(-1)
    onehot = e[:, None] == jnp.arange(N_EXP, dtype=I32)[None, :]
    blk = 256
    oh = onehot.astype(F32).reshape(N_PAIR // blk, blk, N_EXP)
    tri = (jnp.arange(blk)[:, None] >= jnp.arange(blk)[None, :]).astype(F32)
    within = jnp.einsum("ij,bjk->bik", tri, oh)
    blk_tot = within[:, -1, :]
    blk_off = jnp.cumsum(blk_tot, axis=0) - blk_tot
    csum = (within + blk_off[:, None, :]).reshape(N_PAIR, N_EXP)
    rank = jnp.sum(jnp.where(onehot, csum, 0.0), axis=1).astype(I32) - 1
    counts = csum[-1].astype(I32)
    tiles_e = (counts + MOE_TM - 1) // MOE_TM
    tile_end = jnp.cumsum(tiles_e)
    tile_start = tile_end - tiles_e
    slot = tile_start[e] * MOE_TM + rank
    n_tiles = tile_end[-1]
    tok = jnp.tile(jnp.arange(N_TOK, dtype=I32), TOP_K)
    tok_of_slot = jnp.zeros((MOE_ROWS,), I32).at[slot].set(tok, unique_indices=True)
    tid = jnp.arange(MOE_TILES, dtype=I32)
    tile_exp = jnp.minimum(jnp.sum((tid[:, None] >= tile_end[None, :]).astype(I32), axis=1),
                           N_EXP - 1)
    tile_rows = jnp.clip(counts[tile_exp] - MOE_TM * (tid - tile_start[tile_exp]), 0, MOE_TM)
    tile_rows = jnp.where(tid < n_tiles, tile_rows, 0).astype(I32)

    groups_e = (tiles_e + MOE_SUB - 1) // MOE_SUB
    g_end = jnp.cumsum(groups_e)
    g_start = g_end - groups_e
    n_groups = g_end[-1]
    gid = jnp.arange(MOE_GROUPS, dtype=I32)
    used = gid < n_groups
    g_exp = jnp.minimum(jnp.sum((gid[:, None] >= g_end[None, :]).astype(I32), axis=1), N_EXP - 1)
    g_exp = jnp.where(used, g_exp, g_exp[jnp.maximum(n_groups - 1, 0)])
    g_in_e = gid - g_start[g_exp]
    first = tile_start[g_exp] + MOE_SUB * g_in_e
    n_valid = jnp.where(used, jnp.clip(tiles_e[g_exp] - MOE_SUB * g_in_e, 0, MOE_SUB), 0)
    sub = jnp.arange(MOE_SUB, dtype=I32)[None, :]
    x_tile = first[:, None] + jnp.minimum(sub, jnp.maximum(n_valid[:, None] - 1, 0))
    x_tile = jnp.where(used[:, None], x_tile, x_tile[jnp.maximum(n_groups - 1, 0)][None, :])
    tail = n_tiles + (gid[:, None] - n_groups) * MOE_SUB + sub
    o_tile = jnp.where(used[:, None],
                       jnp.where(sub < n_valid[:, None], first[:, None] + sub, -1),
                       jnp.where(tail < MOE_TILES, tail, -1))
    return (slot, tok_of_slot, tile_rows, n_tiles.reshape(1).astype(I32),
            x_tile.reshape(-1).astype(I32),
            o_tile.reshape(-1).astype(I32), n_valid.astype(I32), g_exp.astype(I32))


def _row_copy(src_hbm, row, dst_ref, r, sem):
    return pltpu.make_async_copy(src_hbm.at[pl.ds(row, 1), :], dst_ref.at[pl.ds(r, 1), :], sem)


DISP_RING = 4
DISP_LAG = DISP_RING - 1
DISP_BLK = 8


def _dispatch_kernel(nt_ref, rows_ref, tos_ref, x_hbm, o_ref, buf, sem):
    j = pl.program_id(0)
    nt = nt_ref[0]
    done = j - DISP_LAG

    def row_blocks(t):
        return (rows_ref[t] + DISP_BLK - 1) // DISP_BLK

    @pl.when(j < nt)
    def _():
        b = j & (DISP_RING - 1)

        @pl.when(rows_ref[j] < MOE_TM)
        def _():
            buf[b] = jnp.zeros((MOE_TM, D), F32)

        def issue(blk, carry):
            for p in range(DISP_BLK):
                r = blk * DISP_BLK + p
                _row_copy(x_hbm, tos_ref[0, r], buf.at[b], r, sem.at[b]).start()
            return carry

        lax.fori_loop(0, row_blocks(j), issue, 0)

    @pl.when(jnp.logical_and(done >= 0, done < nt))
    def _():
        b = done & (DISP_RING - 1)

        def wait(blk, carry):
            for p in range(DISP_BLK):
                _row_copy(x_hbm, 0, buf.at[b], blk * DISP_BLK + p, sem.at[b]).wait()
            return carry

        lax.fori_loop(0, row_blocks(done), wait, 0)
        o_ref[...] = buf[b].astype(BF16)

    @pl.when(jnp.logical_and(done >= 0, done >= nt))
    def _():
        o_ref[...] = jnp.zeros_like(o_ref)


def _dispatch(h2, tok_of_slot, n_tiles, tile_rows):
    return pl.pallas_call(
        _dispatch_kernel,
        grid_spec=pltpu.PrefetchScalarGridSpec(
            num_scalar_prefetch=2, grid=(MOE_TILES + DISP_LAG,),
            in_specs=[pl.BlockSpec((None, 1, MOE_TM),
                                   lambda j, nt, rows: (jnp.minimum(j, MOE_TILES - 1), 0, 0),
                                   memory_space=pltpu.SMEM),
                      pl.BlockSpec(memory_space=pl.ANY)],
            out_specs=pl.BlockSpec((MOE_TM, D),
                                   lambda j, nt, rows: (jnp.maximum(j - DISP_LAG, 0), 0)),
            scratch_shapes=[pltpu.VMEM((DISP_RING, MOE_TM, D), F32),
                            pltpu.SemaphoreType.DMA((DISP_RING,))]),
        out_shape=jax.ShapeDtypeStruct((MOE_ROWS, D), BF16),
        compiler_params=_cparams(("arbitrary",), 32),
        name="moe_dispatch",
    )(n_tiles, tile_rows, tok_of_slot.reshape(MOE_TILES, 1, MOE_TM), h2)


def _tile_store(acc_ref, b, i, y_hbm, tile, sem):
    return pltpu.make_async_copy(acc_ref.at[b, i], y_hbm.at[pl.ds(tile * MOE_TM, MOE_TM), :],
                                 sem.at[b])


def _moe_kernel(xt_ref, ot_ref, nv_ref, ge_ref, *refs):
    x_refs = refs[:MOE_SUB]
    wg_ref, wu_ref, wd_ref, bg_ref, bu_ref, bd_ref, y_hbm = refs[MOE_SUB:MOE_SUB + 7]
    wgb_ref, wub_ref, wdb_ref, acc_ref, sem = refs[MOE_SUB + 7:]
    g = pl.program_id(0)
    c = pl.program_id(1)
    nv = nv_ref[g]
    b = g & 1
    last_chunk = c == MOE_NCH - 1

    def wait_group(gw):
        for i in range(MOE_SUB):
            @pl.when(ot_ref[gw * MOE_SUB + i] >= 0)
            def _(i=i):
                _tile_store(acc_ref, gw & 1, i, y_hbm, 0, sem).wait()

    @pl.when(jnp.logical_and(c == 0, g >= 2))
    def _():
        wait_group(g - 2)

    @pl.when(jnp.logical_and(c == 0, nv > 0))
    def _():
        for i in range(MOE_SUB):
            acc_ref[b, i] = jnp.broadcast_to(bd_ref[...], (MOE_TM, D))

    def group_step(m):
        wgb_ref[...] = wg_ref[...].astype(BF16)
        wub_ref[...] = wu_ref[...].astype(BF16)
        wdb_ref[...] = wd_ref[...].astype(BF16)
        for i in range(m):
            x = x_refs[i][...]
            gate = jnp.dot(x, wgb_ref[...], preferred_element_type=F32) + bg_ref[...]
            up = jnp.dot(x, wub_ref[...], preferred_element_type=F32) + bu_ref[...]
            gate = jnp.minimum(gate, LIMIT)
            up = jnp.clip(up, -LIMIT, LIMIT)
            act = (gate * jax.nn.sigmoid(ALPHA * gate) * (up + 1.0)).astype(BF16)
            acc_ref[b, i] += jnp.dot(act, wdb_ref[...], preferred_element_type=F32)

    for m in range(1, MOE_SUB + 1):
        pl.when(nv == m)(functools.partial(group_step, m))

    for i in range(MOE_SUB):
        @pl.when(jnp.logical_and(jnp.logical_and(i >= nv, last_chunk), ot_ref[g * MOE_SUB + i] >= 0))
        def _(i=i):
            acc_ref[b, i] = jnp.zeros((MOE_TM, D), F32)

        @pl.when(jnp.logical_and(last_chunk, ot_ref[g * MOE_SUB + i] >= 0))
        def _(i=i):
            _tile_store(acc_ref, b, i, y_hbm, ot_ref[g * MOE_SUB + i], sem).start()

    @pl.when(jnp.logical_and(last_chunk, g == MOE_GROUPS - 1))
    def _():
        wait_group(g - 1)
        wait_group(g)


def _moe(xs, wg, wu, wd, bg, bu, bd, x_tile, o_tile, n_valid, g_exp):
    def chunk_of(g, c, nv):
        return jnp.where(nv[g] > 0, c, MOE_NCH - 1)

    def x_spec(i):
        return pl.BlockSpec((MOE_TM, D), lambda g, c, xt, ot, nv, ge: (xt[g * MOE_SUB + i], 0))

    w_in_spec = pl.BlockSpec((None, D, MOE_CH),
                             lambda g, c, xt, ot, nv, ge: (ge[g], 0, chunk_of(g, c, nv)))
    b_in_spec = pl.BlockSpec((None, 1, MOE_CH),
                             lambda g, c, xt, ot, nv, ge: (ge[g], 0, chunk_of(g, c, nv)))
    w_out_spec = pl.BlockSpec((None, MOE_CH, D),
                              lambda g, c, xt, ot, nv, ge: (ge[g], chunk_of(g, c, nv), 0))
    b_out_spec = pl.BlockSpec((None, 1, D), lambda g, c, xt, ot, nv, ge: (ge[g], 0, 0))
    return pl.pallas_call(
        _moe_kernel,
        grid_spec=pltpu.PrefetchScalarGridSpec(
            num_scalar_prefetch=4, grid=(MOE_GROUPS, MOE_NCH),
            in_specs=[*[x_spec(i) for i in range(MOE_SUB)],
                      w_in_spec, w_in_spec, w_out_spec, b_in_spec, b_in_spec, b_out_spec],
            out_specs=pl.BlockSpec(memory_space=pl.ANY),
            scratch_shapes=[pltpu.VMEM((D, MOE_CH), BF16), pltpu.VMEM((D, MOE_CH), BF16),
                            pltpu.VMEM((MOE_CH, D), BF16),
                            pltpu.VMEM((2, MOE_SUB, MOE_TM, D), F32),
                            pltpu.SemaphoreType.DMA((2,))]),
        out_shape=jax.ShapeDtypeStruct((MOE_ROWS, D), F32),
        compiler_params=_cparams(("arbitrary", "arbitrary"), 62),
        name="moe_experts",
    )(x_tile, o_tile, n_valid, g_exp, *([xs] * MOE_SUB), wg, wu, wd,
      bg.reshape(N_EXP, 1, D_FF), bu.reshape(N_EXP, 1, D_FF), bd.reshape(N_EXP, 1, D))


COMB_TM = 256


COMB_TILES = N_TOK // COMB_TM


def _combine_kernel(slot_ref, ys_hbm, w_ref, x1_ref, g2_ref, npost_ref, oc_ref, ol_ref, buf, sem):
    i = pl.program_id(0)

    @pl.when(i < COMB_TILES)
    def _():
        b = i & 1
        for kk in range(TOP_K):
            def issue(r, carry, kk=kk):
                _row_copy(ys_hbm, slot_ref[kk, r], buf.at[b, kk], r, sem.at[b]).start()
                return carry

            lax.fori_loop(0, COMB_TM, issue, 0, unroll=8)

    @pl.when(i >= 1)
    def _():
        b = (i - 1) & 1
        for kk in range(TOP_K):
            def wait(r, carry, kk=kk):
                _row_copy(ys_hbm, 0, buf.at[b, kk], r, sem.at[b]).wait()
                return carry

            lax.fori_loop(0, COMB_TM, wait, 0, unroll=8)
        w = w_ref[...]
        f = buf[b, 0] * w[:, 0:1]
        for kk in range(1, TOP_K):
            f = f + buf[b, kk] * w[:, kk:kk + 1]
        out = x1_ref[...] + g2_ref[...] * _rms(f, npost_ref[...])
        base = (i - 1) * COMB_TM

        @pl.when(base < N_CTX)
        def _():
            oc_ref[...] = out

        @pl.when(base >= N_CTX)
        def _():
            ol_ref[...] = out


def _combine(ys, slot, top_w_t, x1, mod3, npost):
    tm = COMB_TM

    def prev(i):
        return jnp.maximum(i - 1, 0)

    oc_spec, ol_spec = _ctx_lat_specs(tm, D)
    return pl.pallas_call(
        _combine_kernel,
        grid=(COMB_TILES + 1,),
        in_specs=[pl.BlockSpec((TOP_K, tm), lambda i: (0, jnp.minimum(i, COMB_TILES - 1)),
                               memory_space=pltpu.SMEM),
                  pl.BlockSpec(memory_space=pl.ANY),
                  pl.BlockSpec((tm, TOP_K), lambda i: (prev(i), 0)),
                  pl.BlockSpec((tm, D), lambda i: (prev(i), 0)),
                  pl.BlockSpec((None, 1, D), lambda i: (_mod_row(prev(i) * tm), 0, MOD_G2)),
                  pl.BlockSpec((1, D), lambda i: (0, 0))],
        out_specs=(pl.BlockSpec((tm, D), lambda i: oc_spec.index_map(prev(i))),
                   pl.BlockSpec((tm, D), lambda i: ol_spec.index_map(prev(i)))),
        scratch_shapes=[pltpu.VMEM((2, TOP_K, tm, D), F32), pltpu.SemaphoreType.DMA((2,))],
        out_shape=(jax.ShapeDtypeStruct((N_CTX, D), F32), jax.ShapeDtypeStruct((N_LAT, D), F32)),
        compiler_params=_cparams(("arbitrary",), 48),
        name="moe_combine",
    )(slot.reshape(TOP_K, N_TOK), ys, top_w_t, x1, mod3, npost.reshape(1, D))


def _rope_tables():
    pos = jnp.arange(DEC_SEQ)
    row = (pos // GRID_W).astype(F32)
    col = (pos % GRID_W).astype(F32)
    n_freq = DK // 4
    freqs = THETA ** (-jnp.arange(n_freq, dtype=F32) / n_freq)
    ar = row[:, None] * freqs
    ac = col[:, None] * freqs
    cos_t = jnp.concatenate([jnp.cos(ar), jnp.cos(ar), jnp.cos(ac), jnp.cos(ac)], axis=1)
    sin_t = jnp.concatenate([-jnp.sin(ar), jnp.sin(ar), -jnp.sin(ac), jnp.sin(ac)], axis=1)
    return cos_t.astype(F32), sin_t.astype(F32)


def kernel(x_prompt, x_sample, state_ret, c, c_ctx, w_ada, b_ada, norm_mix_pre, norm_mix_post,
           norm_ffn_pre, norm_ffn_post, w_in, conv_w, conv_b, ret_decay_logit, ret_gn_w,
           w_br_conv, w_br_ret, w_out, router_w, router_b, moe_w_gate, moe_b_gate,
           moe_w_up, moe_b_up, moe_w_down, moe_b_down):
    layer = 0
    x_ctx = x_prompt.reshape(N_CTX, D)
    x_lat = x_sample.reshape(N_LAT, D)
    cvec = jnp.concatenate(
        [c_ctx[None, :], c, jnp.zeros((MOD_ROWS - 1 - DEC_BATCH, D), F32)], axis=0)

    mod = _modulation(cvec, w_ada[layer], b_ada[layer])
    mod3 = mod.reshape(MOD_ROWS, 1, N_MOD * D)

    h1 = _norm_mod(x_ctx, x_lat, norm_mix_pre[layer], mod3)
    proj = _in_proj(h1, w_in[layer])

    a = _conv_branch(proj, conv_w[layer], conv_b[layer], w_br_conv[layer].astype(BF16))

    cos_t, sin_t = _rope_tables()
    dl = ret_decay_logit[layer].astype(F32)
    r_ctx, new_state = _retention(proj, dl, ret_gn_w[layer], None, None, None, latent=False)
    r_lat = _retention(proj, dl, ret_gn_w[layer], cos_t, sin_t, state_ret, latent=True)

    x1, h2, top_i, top_w = _mix_route(
        r_ctx, r_lat, w_br_ret[layer].astype(BF16), a, proj, w_out[layer].astype(BF16),
        x_ctx, x_lat, mod3, norm_mix_post[layer], norm_ffn_pre[layer],
        router_w[layer].T.astype(BF16), router_b[layer])

    slot, tok_of_slot, tile_rows, n_tiles, x_tile, o_tile, n_valid, g_exp = _routing_tables(top_i)
    xs = _dispatch(h2, tok_of_slot, n_tiles, tile_rows)
    ys = _moe(xs, moe_w_gate[layer], moe_w_up[layer], moe_w_down[layer],
              moe_b_gate[layer], moe_b_up[layer], moe_b_down[layer],
              x_tile, o_tile, n_valid, g_exp)
    y_ctx, y_lat = _combine(ys, slot, top_w.T, x1, mod3, norm_ffn_post[layer])

    return (y_ctx.reshape(BATCH, SEQ, D), y_lat.reshape(DEC_BATCH, DEC_SEQ, D), new_state)
```

```python
import functools

import jax
import jax.numpy as jnp
import numpy as np
from jax import lax
from jax.experimental import pallas as pl
from jax.experimental.pallas import tpu as pltpu

F32 = jnp.float32
BF16 = jnp.bfloat16
I32 = jnp.int32

D = 2048
BATCH = 16
SEQ = 256
DEC_BATCH = 2
DEC_SEQ = 1024
GRID_W = 64
CONV_W = 1024
HEADS = 8
DK = 128
DV = 256
CHUNK = 128
N_EXP = 32
TOP_K = 4
D_FF = 2048
LIMIT = 7.0
ALPHA = 1.702
THETA = 10000.0
EPS = 1e-6
N_MOD = 6
N_CTX = BATCH * SEQ
N_LAT = DEC_BATCH * DEC_SEQ
N_TOK = N_CTX + N_LAT
IN_COLS = 3 * CONV_W + 2 * HEADS * DK + 2 * HEADS * DV + 2 * D
COL_CX, COL_CB, COL_CC = 0, CONV_W, 2 * CONV_W
COL_Q = 3 * CONV_W
COL_K = COL_Q + HEADS * DK
COL_V = COL_K + HEADS * DK
COL_G = COL_V + HEADS * DV
COL_MGC = COL_G + HEADS * DV
COL_MGR = COL_MGC + D
MOD_SH1, MOD_SC1, MOD_G1, MOD_SH2, MOD_SC2, MOD_G2 = range(6)
MOD_ROWS = 8

MOE_TM = 256
N_PAIR = N_TOK * TOP_K
MOE_TILES = (N_PAIR + N_EXP * (MOE_TM - 1)) // MOE_TM + 1
MOE_ROWS = MOE_TILES * MOE_TM
MOE_SUB = 4
MOE_GROUPS = (MOE_TILES + (MOE_SUB - 1) * N_EXP) // MOE_SUB
MOE_CH = 512
MOE_NCH = D_FF // MOE_CH
XS_TILES = MOE_TILES + MOE_SUB - 1

MIB = 1024 * 1024


def _cparams(sem, vmem_mib):
    return pltpu.CompilerParams(dimension_semantics=sem, vmem_limit_bytes=vmem_mib * MIB)


def _mod_row(t0):
    return jnp.where(t0 < N_CTX, 0, 1 + (t0 - N_CTX) // DEC_SEQ)


def _mod_spec(tm, chunk):
    return pl.BlockSpec((None, 1, D), lambda i: (_mod_row(i * tm), 0, chunk))


def _rms(x, g):
    return (x * lax.rsqrt(jnp.mean(x * x, axis=-1, keepdims=True) + EPS)) * g


def _ada_kernel(c_ref, w_ref, b_ref, o_ref):
    c = c_ref[...]
    s = c * jax.nn.sigmoid(c)
    o_ref[...] = jnp.dot(s.astype(BF16), w_ref[...].astype(BF16),
                         preferred_element_type=F32) + b_ref[...]


def _modulation(cvec, w_ada, b_ada):
    tn = 1024
    return pl.pallas_call(
        _ada_kernel,
        grid=(N_MOD * D // tn,),
        in_specs=[pl.BlockSpec((MOD_ROWS, D), lambda j: (0, 0)),
                  pl.BlockSpec((D, tn), lambda j: (0, j)),
                  pl.BlockSpec((1, tn), lambda j: (0, j))],
        out_specs=pl.BlockSpec((MOD_ROWS, tn), lambda j: (0, j)),
        out_shape=jax.ShapeDtypeStruct((MOD_ROWS, N_MOD * D), F32),
        compiler_params=_cparams(("arbitrary",), 40),
        name="ada_modulation",
    )(cvec, w_ada, b_ada.reshape(1, -1))


NORM_TM = 512


def _ctx_lat_specs(tm, width):
    ctx_tiles = N_CTX // tm
    return (pl.BlockSpec((tm, width), lambda i, *_: (jnp.minimum(i, ctx_tiles - 1), 0)),
            pl.BlockSpec((tm, width), lambda i, *_: (jnp.maximum(i - ctx_tiles, 0), 0)))


def _norm_mod_kernel(xc_ref, xl_ref, g_ref, sh_ref, sc_ref, o_ref):
    x = jnp.where(pl.program_id(0) * NORM_TM < N_CTX, xc_ref[...], xl_ref[...])
    y = _rms(x, g_ref[...])
    o_ref[...] = (y * (1.0 + sc_ref[...]) + sh_ref[...]).astype(BF16)


def _norm_mod(x_ctx, x_lat, g, mod3):
    tm = NORM_TM
    return pl.pallas_call(
        _norm_mod_kernel,
        grid=(N_TOK // tm,),
        in_specs=[*_ctx_lat_specs(tm, D),
                  pl.BlockSpec((1, D), lambda i: (0, 0)),
                  _mod_spec(tm, MOD_SH1),
                  _mod_spec(tm, MOD_SC1)],
        out_specs=pl.BlockSpec((tm, D), lambda i: (i, 0)),
        out_shape=jax.ShapeDtypeStruct((N_TOK, D), BF16),
        compiler_params=_cparams(("arbitrary",), 40),
        name="norm_modulate",
    )(x_ctx, x_lat, g.reshape(1, D), mod3, mod3)


def _proj_kernel(h_ref, w_ref, o_ref, wb_ref):
    @pl.when(pl.program_id(1) == 0)
    def _():
        wb_ref[...] = w_ref[...].astype(BF16)

    o_ref[...] = jnp.dot(h_ref[...], wb_ref[...], preferred_element_type=F32).astype(BF16)


def _in_proj(h, w_in):
    tm, tn = 1024, 1024
    return pl.pallas_call(
        _proj_kernel,
        grid=(IN_COLS // tn, N_TOK // tm),
        in_specs=[pl.BlockSpec((tm, D), lambda n, m: (m, 0)),
                  pl.BlockSpec((D, tn), lambda n, m: (0, n))],
        out_specs=pl.BlockSpec((tm, tn), lambda n, m: (m, n)),
        out_shape=jax.ShapeDtypeStruct((N_TOK, IN_COLS), BF16),
        scratch_shapes=[pltpu.VMEM((D, tn), BF16)],
        compiler_params=_cparams(("arbitrary", "arbitrary"), 48),
        name="in_proj",
    )(h, w_in)


CONV_TM = 256


def _conv_kernel(cx_ref, cb_ref, cc_ref, mg0_ref, mg1_ref, cw_ref, cbias_ref, w_ref, o_ref):
    i = pl.program_id(0)
    u = cc_ref[...].astype(F32) * cx_ref[...].astype(F32)
    seg = jnp.where(i * CONV_TM < N_CTX, SEQ, GRID_W)
    pos = lax.broadcasted_iota(I32, (CONV_TM, 1), 0) & (seg - 1)
    prev = jnp.where(pos == 0, 0.0, pltpu.roll(u, 1, 0))
    nxt = jnp.where(pos == seg - 1, 0.0, pltpu.roll(u, CONV_TM - 1, 0))
    cw = cw_ref[...]
    uc = prev * cw[0:1, :] + u * cw[1:2, :] + nxt * cw[2:3, :] + cbias_ref[...]
    z = (cb_ref[...].astype(F32) * uc).astype(BF16)
    y = jnp.dot(z, w_ref[...], preferred_element_type=F32)
    mg = jnp.concatenate([mg0_ref[...], mg1_ref[...]], axis=1).astype(F32)
    o_ref[...] = (jax.nn.sigmoid(mg) * y).astype(BF16)


def _conv_branch(proj, conv_w, conv_b, w_br_conv_bf):
    tm = CONV_TM
    cw = 1024

    def col(c):
        return pl.BlockSpec((tm, cw), lambda i: (i, c))

    return pl.pallas_call(
        _conv_kernel,
        grid=(N_TOK // tm,),
        in_specs=[col(COL_CX // cw), col(COL_CB // cw), col(COL_CC // cw),
                  col(COL_MGC // cw), col(COL_MGC // cw + 1),
                  pl.BlockSpec((3, CONV_W), lambda i: (0, 0)),
                  pl.BlockSpec((1, CONV_W), lambda i: (0, 0)),
                  pl.BlockSpec((CONV_W, D), lambda i: (0, 0))],
        out_specs=pl.BlockSpec((tm, D), lambda i: (i, 0)),
        out_shape=jax.ShapeDtypeStruct((N_TOK, D), BF16),
        compiler_params=_cparams(("arbitrary",), 48),
        name="conv_branch",
    )(proj, proj, proj, proj, proj, conv_w, conv_b.reshape(1, CONV_W), w_br_conv_bf)


def _log_sigmoid(x):
    return -(jnp.maximum(-x, 0.0) + jnp.log1p(jnp.exp(-jnp.abs(x))))


def _rope(x, cos, sin_signed):
    lane = lax.broadcasted_iota(I32, x.shape, 1)
    swapped = jnp.where((lane & 63) < 32, pltpu.roll(x, 96, 1), pltpu.roll(x, 32, 1))
    return x * cos + swapped * sin_signed


def _retention_kernel(*refs, seq_len, latent, hps):
    for hh in range(hps):
        _retention_head(refs, pl.program_id(1) * hps + hh, hh, seq_len=seq_len, latent=latent)


def _retention_head(refs, h, hh, *, seq_len, latent):
    if latent:
        (dl_ref, q_ref, k_ref, v_ref, g_ref, gn_ref, cos_ref, sin_ref, s0_ref,
         r_ref, o_acc) = refs
    else:
        dl_ref, q_ref, k_ref, v_ref, g_ref, gn_ref, r_ref, sout_ref, o_acc = refs
    c = CHUNK
    n_chunks = seq_len // c
    kcol = slice(hh * DK, (hh + 1) * DK)
    vcol = slice(hh * DV, (hh + 1) * DV)

    lgf = _log_sigmoid(jnp.full((c, c), dl_ref[0, h], F32))
    lgb = _log_sigmoid(jnp.full((c, c), dl_ref[1, h], F32))
    lgf_w = _log_sigmoid(jnp.full((c, DV), dl_ref[0, h], F32))
    lgb_w = _log_sigmoid(jnp.full((c, DV), dl_ref[1, h], F32))

    ii = lax.broadcasted_iota(I32, (c, c), 0).astype(F32)
    jj = lax.broadcasted_iota(I32, (c, c), 1).astype(F32)
    diff = ii - jj
    m_in = jnp.where(diff >= 0.0, jnp.exp(jnp.maximum(diff, 0.0) * lgf),
                     jnp.exp(jnp.maximum(-diff, 0.0) * lgb))
    io = lax.broadcasted_iota(I32, (c, DV), 0).astype(F32)
    dq_f = jnp.exp((io + 1.0) * lgf_w)
    dq_b = jnp.exp((c - io) * lgb_w)
    dk_f = jnp.exp((c - 1.0 - jj) * lgf)
    dk_b = jnp.exp(jj * lgb)
    dc_f = jnp.exp(c * lgf_w)
    dc_b = jnp.exp(c * lgb_w)

    q = q_ref[:, kcol].astype(F32)
    k = k_ref[:, kcol].astype(F32) * (DK ** -0.5)
    if latent:
        q = _rope(q, cos_ref[...], sin_ref[...])
        k = _rope(k, cos_ref[...], sin_ref[...])
        s_f = s0_ref[0, hh]
        s_b = s0_ref[1, hh]
    else:
        s_f = jnp.zeros((DK, DV), F32)
        s_b = jnp.zeros((DK, DV), F32)
    qb = q.astype(BF16)

    def chunk(x, n):
        return x[n * c:(n + 1) * c]

    for n in range(n_chunks):
        qn = chunk(qb, n)
        kt = chunk(k, n).T
        vn = v_ref[n * c:(n + 1) * c, vcol]
        s = jnp.dot(qn, kt.astype(BF16), preferred_element_type=F32)
        p = (s * m_in).astype(BF16)
        o = jnp.dot(p, vn, preferred_element_type=F32)
        o = o + jnp.dot(qn, s_f.astype(BF16), preferred_element_type=F32) * dq_f
        o_acc[n * c:(n + 1) * c, vcol] = o
        s_f = dc_f * s_f + jnp.dot((kt * dk_f).astype(BF16), vn, preferred_element_type=F32)

    for n in reversed(range(n_chunks)):
        qn = chunk(qb, n)
        kt = chunk(k, n).T
        vn = v_ref[n * c:(n + 1) * c, vcol]
        o_acc[n * c:(n + 1) * c, vcol] += (
            jnp.dot(qn, s_b.astype(BF16), preferred_element_type=F32) * dq_b)
        s_b = dc_b * s_b + jnp.dot((kt * dk_b).astype(BF16), vn, preferred_element_type=F32)

    if not latent:
        sout_ref[0, hh] = s_f
        sout_ref[1, hh] = s_b

    o = o_acc[:, vcol]
    mu = jnp.mean(o, axis=-1, keepdims=True)
    oc = o - mu
    var = jnp.mean(oc * oc, axis=-1, keepdims=True)
    on = oc * lax.rsqrt(var + EPS) * gn_ref[:, vcol]
    g = g_ref[:, vcol].astype(F32)
    r_ref[:, vcol] = ((g * jax.nn.sigmoid(g)) * on).astype(BF16)


def _retention(proj, decay_logit, gn_w, cos_t, sin_t, state0, *, latent):
    seq_len = DEC_SEQ if latent else SEQ
    nb = DEC_BATCH if latent else BATCH
    rb0 = (N_CTX // DEC_SEQ) if latent else 0

    hps = 1 if latent else 4
    assert COL_V % (DV * hps) == 0 and COL_G % (DV * hps) == 0 and COL_Q % (DK * hps) == 0

    def spec(width, col0):
        w = width * hps
        return pl.BlockSpec((seq_len, w), lambda b, h, dl: (rb0 + b, col0 // w + h))

    in_specs = [spec(DK, COL_Q), spec(DK, COL_K), spec(DV, COL_V), spec(DV, COL_G),
                pl.BlockSpec((1, DV * hps), lambda b, h, dl: (0, h))]
    args = [proj, proj, proj, proj, gn_w.reshape(1, HEADS * DV)]
    state_spec = pl.BlockSpec((None, None, 2, hps, DK, DV), lambda b, h, dl: (b, 0, 0, h, 0, 0))
    r_shape = jax.ShapeDtypeStruct((nb * seq_len, HEADS * DV), BF16)
    r_spec = pl.BlockSpec((seq_len, DV * hps), lambda b, h, dl: (b, h))
    if latent:
        in_specs += [pl.BlockSpec((seq_len, DK), lambda b, h, dl: (0, 0)),
                     pl.BlockSpec((seq_len, DK), lambda b, h, dl: (0, 0)),
                     state_spec]
        args += [cos_t, sin_t, state0]
        out_specs, out_shape = r_spec, r_shape
    else:
        out_specs = (r_spec, state_spec)
        out_shape = (r_shape, jax.ShapeDtypeStruct((BATCH, 1, 2, HEADS, DK, DV), F32))
    return pl.pallas_call(
        functools.partial(_retention_kernel, seq_len=seq_len, latent=latent, hps=hps),
        grid_spec=pltpu.PrefetchScalarGridSpec(
            num_scalar_prefetch=1, grid=(nb, HEADS // hps), in_specs=in_specs,
            out_specs=out_specs, scratch_shapes=[pltpu.VMEM((seq_len, DV * hps), F32)]),
        out_shape=out_shape,
        compiler_params=_cparams(("arbitrary", "arbitrary"), 40),
        name="retention_latent" if latent else "retention_context",
    )(decay_logit, *args)


MIX_TM = 256


def _mix_kernel(rc_ref, rl_ref, wbr_ref, a_ref, mg0_ref, mg1_ref, wout_ref, xc_ref, xl_ref, g1_ref,
                sc2_ref, sh2_ref, npost_ref, npre_ref, rwt_ref, rb_ref,
                x1_ref, h2_ref, ti_ref, tw_ref):
    is_ctx = pl.program_id(0) * MIX_TM < N_CTX
    r = jnp.where(is_ctx, rc_ref[...], rl_ref[...])
    x = jnp.where(is_ctx, xc_ref[...], xl_ref[...])
    y = jnp.dot(r, wbr_ref[...], preferred_element_type=F32)
    mg = jnp.concatenate([mg0_ref[...], mg1_ref[...]], axis=1).astype(F32)
    mix = a_ref[...].astype(F32) + jax.nn.sigmoid(mg) * y
    out = jnp.dot(mix.astype(BF16), wout_ref[...], preferred_element_type=F32)
    x1 = x + g1_ref[...] * _rms(out, npost_ref[...])
    x1_ref[...] = x1
    h2 = _rms(x1, npre_ref[...]) * (1.0 + sc2_ref[...]) + sh2_ref[...]
    h2_ref[...] = h2
    logits = lax.dot_general(rwt_ref[...], h2.astype(BF16), (((1,), (1,)), ((), ())),
                             preferred_element_type=F32) + rb_ref[...]
    eidx = lax.broadcasted_iota(I32, logits.shape, 0).astype(F32)
    vals, idxs = [], []
    for _ in range(TOP_K):
        m = jnp.max(logits, axis=0, keepdims=True)
        am = jnp.min(jnp.where(logits == m, eidx, float(N_EXP)), axis=0, keepdims=True)
        vals.append(m)
        idxs.append(am)
        logits = jnp.where(eidx == am, -jnp.inf, logits)
    es = [jnp.exp(v - vals[0]) for v in vals]
    denom = es[0] + es[1] + es[2] + es[3]
    for kk in range(TOP_K):
        ti_ref[kk:kk + 1, :] = idxs[kk].astype(I32)
        tw_ref[kk:kk + 1, :] = es[kk] / denom


def _mix_route(r_ctx, r_lat, wbr_bf, a, proj, wout_bf, x_ctx, x_lat, mod3, npost, npre, rwt_bf, rb):
    tm = MIX_TM
    cw = 1024

    def full(shape):
        return pl.BlockSpec(shape, lambda i: (0,) * len(shape), pipeline_mode=pl.Buffered(1))

    def row(width):
        return pl.BlockSpec((tm, width), lambda i: (i, 0))

    return pl.pallas_call(
        _mix_kernel,
        grid=(N_TOK // tm,),
        in_specs=[*_ctx_lat_specs(tm, D),
                  full((D, D)), row(D),
                  pl.BlockSpec((tm, cw), lambda i: (i, COL_MGR // cw)),
                  pl.BlockSpec((tm, cw), lambda i: (i, COL_MGR // cw + 1)),
                  full((D, D)), *_ctx_lat_specs(tm, D),
                  _mod_spec(tm, MOD_G1), _mod_spec(tm, MOD_SC2), _mod_spec(tm, MOD_SH2),
                  full((1, D)), full((1, D)), full((N_EXP, D)), full((N_EXP, 1))],
        out_specs=(row(D), row(D),
                   pl.BlockSpec((TOP_K, tm), lambda i: (0, i)),
                   pl.BlockSpec((TOP_K, tm), lambda i: (0, i))),
        out_shape=(jax.ShapeDtypeStruct((N_TOK, D), F32),
                   jax.ShapeDtypeStruct((N_TOK, D), F32),
                   jax.ShapeDtypeStruct((TOP_K, N_TOK), I32),
                   jax.ShapeDtypeStruct((TOP_K, N_TOK), F32)),
        compiler_params=_cparams(("arbitrary",), 56),
        name="mix_out_route",
    )(r_ctx, r_lat, wbr_bf, a, proj, proj, wout_bf, x_ctx, x_lat, mod3, mod3, mod3,
      npost.reshape(1, D), npre.reshape(1, D), rwt_bf, rb.reshape(N_EXP, 1))


def _routing_tables(top_i):
    e = top_i.reshape(-1)
    onehot = e[:, None] == jnp.arange(N_EXP, dtype=I32)[None, :]
    blk = 256
    oh = onehot.astype(F32).reshape(N_PAIR // blk, blk, N_EXP)
    tri = (jnp.arange(blk)[:, None] >= jnp.arange(blk)[None, :]).astype(F32)
    within = jnp.einsum("ij,bjk->bik", tri, oh)
    blk_tot = within[:, -1, :]
    blk_off = jnp.cumsum(blk_tot, axis=0) - blk_tot
    csum = (within + blk_off[:, None, :]).reshape(N_PAIR, N_EXP)
    rank = jnp.sum(jnp.where(onehot, csum, 0.0), axis=1).astype(I32) - 1
    counts = csum[-1].astype(I32)
    tiles_e = (counts + MOE_TM - 1) // MOE_TM
    tile_end = jnp.cumsum(tiles_e)
    tile_start = tile_end - tiles_e
    slot = tile_start[e] * MOE_TM + rank
    n_tiles = tile_end[-1]
    tok = jnp.tile(jnp.arange(N_TOK, dtype=I32), TOP_K)
    tok_of_slot = jnp.zeros((MOE_ROWS,), I32).at[slot].set(tok, unique_indices=True)
    tid = jnp.arange(MOE_TILES, dtype=I32)
    tile_exp = jnp.minimum(jnp.sum((tid[:, None] >= tile_end[None, :]).astype(I32), axis=1),
                           N_EXP - 1)
    tile_rows = jnp.clip(counts[tile_exp] - MOE_TM * (tid - tile_start[tile_exp]), 0, MOE_TM)
    tile_rows = jnp.where(tid < n_tiles, tile_rows, 0).astype(I32)

    groups_e = (tiles_e + MOE_SUB - 1) // MOE_SUB
    g_end = jnp.cumsum(groups_e)
    g_start = g_end - groups_e
    n_groups = g_end[-1]
    gid = jnp.arange(MOE_GROUPS, dtype=I32)
    used = gid < n_groups
    g_exp = jnp.minimum(jnp.sum((gid[:, None] >= g_end[None, :]).astype(I32), axis=1), N_EXP - 1)
    g_exp = jnp.where(used, g_exp, g_exp[jnp.maximum(n_groups - 1, 0)])
    g_in_e = gid - g_start[g_exp]
    first = tile_start[g_exp] + MOE_SUB * g_in_e
    n_valid = jnp.where(used, jnp.clip(tiles_e[g_exp] - MOE_SUB * g_in_e, 0, MOE_SUB), 0)
    sub = jnp.arange(MOE_SUB, dtype=I32)[None, :]
    x_tile = first[:, None] + jnp.minimum(sub, jnp.maximum(n_valid[:, None] - 1, 0))
    x_tile = jnp.where(used[:, None], x_tile, x_tile[jnp.maximum(n_groups - 1, 0)][None, :])
    tail = n_tiles + (gid[:, None] - n_groups) * MOE_SUB + sub
    o_tile = jnp.where(used[:, None],
                       jnp.where(sub < n_valid[:, None], first[:, None] + sub, -1),
                       jnp.where(tail < MOE_TILES, tail, -1))
    return (slot, tok_of_slot, tile_rows, n_tiles.reshape(1).astype(I32),
            x_tile.reshape(-1).astype(I32),
            o_tile.reshape(-1).astype(I32), n_valid.astype(I32), g_exp.astype(I32))


def _row_copy(src_hbm, row, dst_ref, r, sem):
    return pltpu.make_async_copy(src_hbm.at[pl.ds(row, 1), :], dst_ref.at[pl.ds(r, 1), :], sem)


DISP_RING = 4
DISP_LAG = DISP_RING - 1
DISP_BLK = 8


def _dispatch_kernel(nt_ref, rows_ref, tos_ref, x_hbm, o_ref, buf, sem):
    j = pl.program_id(0)
    nt = nt_ref[0]
    done = j - DISP_LAG

    def row_blocks(t):
        return (rows_ref[t] + DISP_BLK - 1) // DISP_BLK

    @pl.when(j < nt)
    def _():
        b = j & (DISP_RING - 1)

        @pl.when(rows_ref[j] < MOE_TM)
        def _():
            buf[b] = jnp.zeros((MOE_TM, D), F32)

        def issue(blk, carry):
            for p in range(DISP_BLK):
                r = blk * DISP_BLK + p
                _row_copy(x_hbm, tos_ref[0, r], buf.at[b], r, sem.at[b]).start()
            return carry

        lax.fori_loop(0, row_blocks(j), issue, 0)

    @pl.when(jnp.logical_and(done >= 0, done < nt))
    def _():
        b = done & (DISP_RING - 1)

        def wait(blk, carry):
            for p in range(DISP_BLK):
                _row_copy(x_hbm, 0, buf.at[b], blk * DISP_BLK + p, sem.at[b]).wait()
            return carry

        lax.fori_loop(0, row_blocks(done), wait, 0)
        o_ref[...] = buf[b].astype(BF16)

    @pl.when(jnp.logical_and(done >= 0, done >= nt))
    def _():
        o_ref[...] = jnp.zeros_like(o_ref)


def _dispatch(h2, tok_of_slot, n_tiles, tile_rows):
    return pl.pallas_call(
        _dispatch_kernel,
        grid_spec=pltpu.PrefetchScalarGridSpec(
            num_scalar_prefetch=2, grid=(XS_TILES + DISP_LAG,),
            in_specs=[pl.BlockSpec((None, 1, MOE_TM),
                                   lambda j, nt, rows: (jnp.minimum(j, MOE_TILES - 1), 0, 0),
                                   memory_space=pltpu.SMEM),
                      pl.BlockSpec(memory_space=pl.ANY)],
            out_specs=pl.BlockSpec((MOE_TM, D),
                                   lambda j, nt, rows: (jnp.maximum(j - DISP_LAG, 0), 0)),
            scratch_shapes=[pltpu.VMEM((DISP_RING, MOE_TM, D), F32),
                            pltpu.SemaphoreType.DMA((DISP_RING,))]),
        out_shape=jax.ShapeDtypeStruct((XS_TILES * MOE_TM, D), BF16),
        compiler_params=_cparams(("arbitrary",), 32),
        name="moe_dispatch",
    )(n_tiles, tile_rows, tok_of_slot.reshape(MOE_TILES, 1, MOE_TM), h2)


def _tile_store(acc_ref, b, i, y_hbm, tile, sem):
    return pltpu.make_async_copy(acc_ref.at[b, i], y_hbm.at[pl.ds(tile * MOE_TM, MOE_TM), :],
                                 sem.at[b])


def _moe_kernel(xt_ref, ot_ref, nv_ref, ge_ref, *refs):
    (x_ref, wg_ref, wu_ref, wd_ref, bg_ref, bu_ref, bd_ref, y_hbm,
     wgb_ref, wub_ref, wdb_ref, acc_ref, sem) = refs
    g = pl.program_id(0)
    c = pl.program_id(1)
    nv = nv_ref[g]
    b = g & 1
    last_chunk = c == MOE_NCH - 1

    def wait_group(gw):
        for i in range(MOE_SUB):
            @pl.when(ot_ref[gw * MOE_SUB + i] >= 0)
            def _(i=i):
                _tile_store(acc_ref, gw & 1, i, y_hbm, 0, sem).wait()

    @pl.when(jnp.logical_and(c == 0, g >= 2))
    def _():
        wait_group(g - 2)

    @pl.when(jnp.logical_and(c == 0, nv > 0))
    def _():
        for i in range(MOE_SUB):
            acc_ref[b, i] = jnp.broadcast_to(bd_ref[...], (MOE_TM, D))

    def group_step(m):
        wgb_ref[...] = wg_ref[...].astype(BF16)
        wub_ref[...] = wu_ref[...].astype(BF16)
        wdb_ref[...] = wd_ref[...].astype(BF16)
        x = x_ref[:m * MOE_TM]
        gate = jnp.dot(x, wgb_ref[...], preferred_element_type=F32) + bg_ref[...]
        up = jnp.dot(x, wub_ref[...], preferred_element_type=F32) + bu_ref[...]
        gate = jnp.minimum(gate, LIMIT)
        up = jnp.clip(up, -LIMIT, LIMIT)
        act = (gate * jax.nn.sigmoid(ALPHA * gate) * (up + 1.0)).astype(BF16)
        for i in range(m):
            acc_ref[b, i] += jnp.dot(act[i * MOE_TM:(i + 1) * MOE_TM], wdb_ref[...],
                                     preferred_element_type=F32)

    for m in range(1, MOE_SUB + 1):
        pl.when(nv == m)(functools.partial(group_step, m))

    for i in range(MOE_SUB):
        @pl.when(jnp.logical_and(jnp.logical_and(i >= nv, last_chunk), ot_ref[g * MOE_SUB + i] >= 0))
        def _(i=i):
            acc_ref[b, i] = jnp.zeros((MOE_TM, D), F32)

        @pl.when(jnp.logical_and(last_chunk, ot_ref[g * MOE_SUB + i] >= 0))
        def _(i=i):
            _tile_store(acc_ref, b, i, y_hbm, ot_ref[g * MOE_SUB + i], sem).start()

    @pl.when(jnp.logical_and(last_chunk, g == MOE_GROUPS - 1))
    def _():
        wait_group(g - 1)
        wait_group(g)


def _moe(xs, wg, wu, wd, bg, bu, bd, x_tile, o_tile, n_valid, g_exp):
    def chunk_of(g, c, nv):
        return jnp.where(nv[g] > 0, c, MOE_NCH - 1)

    x_spec = pl.BlockSpec((pl.Element(MOE_SUB * MOE_TM), pl.Element(D)),
                          lambda g, c, xt, ot, nv, ge: (xt[g * MOE_SUB] * MOE_TM, 0))

    w_in_spec = pl.BlockSpec((None, D, MOE_CH),
                             lambda g, c, xt, ot, nv, ge: (ge[g], 0, chunk_of(g, c, nv)))
    b_in_spec = pl.BlockSpec((None, 1, MOE_CH),
                             lambda g, c, xt, ot, nv, ge: (ge[g], 0, chunk_of(g, c, nv)))
    w_out_spec = pl.BlockSpec((None, MOE_CH, D),
                              lambda g, c, xt, ot, nv, ge: (ge[g], chunk_of(g, c, nv), 0))
    b_out_spec = pl.BlockSpec((None, 1, D), lambda g, c, xt, ot, nv, ge: (ge[g], 0, 0))
    return pl.pallas_call(
        _moe_kernel,
        grid_spec=pltpu.PrefetchScalarGridSpec(
            num_scalar_prefetch=4, grid=(MOE_GROUPS, MOE_NCH),
            in_specs=[x_spec, w_in_spec, w_in_spec, w_out_spec, b_in_spec, b_in_spec, b_out_spec],
            out_specs=pl.BlockSpec(memory_space=pl.ANY),
            scratch_shapes=[pltpu.VMEM((D, MOE_CH), BF16), pltpu.VMEM((D, MOE_CH), BF16),
                            pltpu.VMEM((MOE_CH, D), BF16),
                            pltpu.VMEM((2, MOE_SUB, MOE_TM, D), F32),
                            pltpu.SemaphoreType.DMA((2,))]),
        out_shape=jax.ShapeDtypeStruct((MOE_ROWS, D), F32),
        compiler_params=_cparams(("arbitrary", "arbitrary"), 62),
        name="moe_experts",
    )(x_tile, o_tile, n_valid, g_exp, xs, wg, wu, wd,
      bg.reshape(N_EXP, 1, D_FF), bu.reshape(N_EXP, 1, D_FF), bd.reshape(N_EXP, 1, D))


COMB_TM = 256


COMB_TILES = N_TOK // COMB_TM
COMB_BLK = 8


def _combine_kernel(slot_ref, ys_hbm, w_ref, x1_ref, g2_ref, npost_ref, oc_ref, ol_ref, buf, sem):
    i = pl.program_id(0)

    def start_tile(b):
        def issue(blk, carry):
            for kk in range(TOP_K):
                for p in range(COMB_BLK):
                    r = blk * COMB_BLK + p
                    _row_copy(ys_hbm, slot_ref[0, kk * COMB_TM + r], buf.at[b, kk], r,
                              sem.at[b]).start()
            return carry

        lax.fori_loop(0, COMB_TM // COMB_BLK, issue, 0)

    def wait_tile(b):
        def wait(blk, carry):
            for kk in range(TOP_K):
                for p in range(COMB_BLK):
                    _row_copy(ys_hbm, 0, buf.at[b, kk], blk * COMB_BLK + p, sem.at[b]).wait()
            return carry

        lax.fori_loop(0, COMB_TM // COMB_BLK, wait, 0)

    for par in range(2):
        @pl.when(jnp.logical_and(i < COMB_TILES, (i & 1) == par))
        def _(par=par):
            start_tile(par)

    for par in range(2):
        @pl.when(jnp.logical_and(i >= 1, ((i - 1) & 1) == par))
        def _(par=par):
            wait_tile(par)

    @pl.when(i >= 1)
    def _():
        b = (i - 1) & 1
        w = w_ref[...]
        f = buf[b, 0] * w[:, 0:1]
        for kk in range(1, TOP_K):
            f = f + buf[b, kk] * w[:, kk:kk + 1]
        out = x1_ref[...] + g2_ref[...] * _rms(f, npost_ref[...])
        base = (i - 1) * COMB_TM

        @pl.when(base < N_CTX)
        def _():
            oc_ref[...] = out

        @pl.when(base >= N_CTX)
        def _():
            ol_ref[...] = out


def _combine(ys, slot, top_w_t, x1, mod3, npost):
    tm = COMB_TM

    def prev(i):
        return jnp.maximum(i - 1, 0)

    oc_spec, ol_spec = _ctx_lat_specs(tm, D)
    return pl.pallas_call(
        _combine_kernel,
        grid=(COMB_TILES + 1,),
        in_specs=[pl.BlockSpec((None, 1, TOP_K * tm),
                               lambda i: (jnp.minimum(i, COMB_TILES - 1), 0, 0),
                               memory_space=pltpu.SMEM),
                  pl.BlockSpec(memory_space=pl.ANY),
                  pl.BlockSpec((tm, TOP_K), lambda i: (prev(i), 0)),
                  pl.BlockSpec((tm, D), lambda i: (prev(i), 0)),
                  pl.BlockSpec((None, 1, D), lambda i: (_mod_row(prev(i) * tm), 0, MOD_G2)),
                  pl.BlockSpec((1, D), lambda i: (0, 0))],
        out_specs=(pl.BlockSpec((tm, D), lambda i: oc_spec.index_map(prev(i))),
                   pl.BlockSpec((tm, D), lambda i: ol_spec.index_map(prev(i)))),
        scratch_shapes=[pltpu.VMEM((2, TOP_K, tm, D), F32), pltpu.SemaphoreType.DMA((2,))],
        out_shape=(jax.ShapeDtypeStruct((N_CTX, D), F32), jax.ShapeDtypeStruct((N_LAT, D), F32)),
        compiler_params=_cparams(("arbitrary",), 48),
        name="moe_combine",
    )(slot.reshape(TOP_K, COMB_TILES, tm).transpose(1, 0, 2).reshape(COMB_TILES, 1, TOP_K * tm),
      ys, top_w_t, x1, mod3, npost.reshape(1, D))


def _rope_tables():
    pos = jnp.arange(DEC_SEQ)
    row = (pos // GRID_W).astype(F32)
    col = (pos % GRID_W).astype(F32)
    n_freq = DK // 4
    freqs = THETA ** (-jnp.arange(n_freq, dtype=F32) / n_freq)
    ar = row[:, None] * freqs
    ac = col[:, None] * freqs
    cos_t = jnp.concatenate([jnp.cos(ar), jnp.cos(ar), jnp.cos(ac), jnp.cos(ac)], axis=1)
    sin_t = jnp.concatenate([-jnp.sin(ar), jnp.sin(ar), -jnp.sin(ac), jnp.sin(ac)], axis=1)
    return cos_t.astype(F32), sin_t.astype(F32)


def kernel(x_prompt, x_sample, state_ret, c, c_ctx, w_ada, b_ada, norm_mix_pre, norm_mix_post,
           norm_ffn_pre, norm_ffn_post, w_in, conv_w, conv_b, ret_decay_logit, ret_gn_w,
           w_br_conv, w_br_ret, w_out, router_w, router_b, moe_w_gate, moe_b_gate,
           moe_w_up, moe_b_up, moe_w_down, moe_b_down):
    layer = 0
    x_ctx = x_prompt.reshape(N_CTX, D)
    x_lat = x_sample.reshape(N_LAT, D)
    cvec = jnp.concatenate(
        [c_ctx[None, :], c, jnp.zeros((MOD_ROWS - 1 - DEC_BATCH, D), F32)], axis=0)

    mod = _modulation(cvec, w_ada[layer], b_ada[layer])
    mod3 = mod.reshape(MOD_ROWS, 1, N_MOD * D)

    h1 = _norm_mod(x_ctx, x_lat, norm_mix_pre[layer], mod3)
    proj = _in_proj(h1, w_in[layer])

    a = _conv_branch(proj, conv_w[layer], conv_b[layer], w_br_conv[layer].astype(BF16))

    cos_t, sin_t = _rope_tables()
    dl = ret_decay_logit[layer].astype(F32)
    r_ctx, new_state = _retention(proj, dl, ret_gn_w[layer], None, None, None, latent=False)
    r_lat = _retention(proj, dl, ret_gn_w[layer], cos_t, sin_t, state_ret, latent=True)

    x1, h2, top_i, top_w = _mix_route(
        r_ctx, r_lat, w_br_ret[layer].astype(BF16), a, proj, w_out[layer].astype(BF16),
        x_ctx, x_lat, mod3, norm_mix_post[layer], norm_ffn_pre[layer],
        router_w[layer].T.astype(BF16), router_b[layer])

    slot, tok_of_slot, tile_rows, n_tiles, x_tile, o_tile, n_valid, g_exp = _routing_tables(top_i)
    xs = _dispatch(h2, tok_of_slot, n_tiles, tile_rows)
    ys = _moe(xs, moe_w_gate[layer], moe_w_up[layer], moe_w_down[layer],
              moe_b_gate[layer], moe_b_up[layer], moe_b_down[layer],
              x_tile, o_tile, n_valid, g_exp)
    y_ctx, y_lat = _combine(ys, slot, top_w.T, x1, mod3, norm_ffn_post[layer])

    return (y_ctx.reshape(BATCH, SEQ, D), y_lat.reshape(DEC_BATCH, DEC_SEQ, D), new_state)
```

```python
import functools

import jax
import jax.numpy as jnp
import numpy as np
from jax import lax
from jax.experimental import pallas as pl
from jax.experimental.pallas import tpu as pltpu

F32 = jnp.float32
BF16 = jnp.bfloat16
I32 = jnp.int32

D = 2048
BATCH = 16
SEQ = 256
DEC_BATCH = 2
DEC_SEQ = 1024
GRID_W = 64
CONV_W = 1024
HEADS = 8
DK = 128
DV = 256
CHUNK = 128
N_EXP = 32
TOP_K = 4
D_FF = 2048
LIMIT = 7.0
ALPHA = 1.702
THETA = 10000.0
EPS = 1e-6
N_MOD = 6
N_CTX = BATCH * SEQ
N_LAT = DEC_BATCH * DEC_SEQ
N_TOK = N_CTX + N_LAT
IN_COLS = 3 * CONV_W + 2 * HEADS * DK + 2 * HEADS * DV + 2 * D
COL_CX, COL_CB, COL_CC = 0, CONV_W, 2 * CONV_W
COL_Q = 3 * CONV_W
COL_K = COL_Q + HEADS * DK
COL_V = COL_K + HEADS * DK
COL_G = COL_V + HEADS * DV
COL_MGC = COL_G + HEADS * DV
COL_MGR = COL_MGC + D
MOD_SH1, MOD_SC1, MOD_G1, MOD_SH2, MOD_SC2, MOD_G2 = range(6)
MOD_ROWS = 8

MOE_TM = 256
N_PAIR = N_TOK * TOP_K
MOE_TILES = (N_PAIR + N_EXP * (MOE_TM - 1)) // MOE_TM + 1
MOE_ROWS = MOE_TILES * MOE_TM
MOE_SUB = 4
MOE_GROUPS = (MOE_TILES + (MOE_SUB - 1) * N_EXP) // MOE_SUB
MOE_CH = 512
MOE_NCH = D_FF // MOE_CH
XS_TILES = MOE_TILES + MOE_SUB - 1

MIB = 1024 * 1024


def _cparams(sem, vmem_mib):
    return pltpu.CompilerParams(dimension_semantics=sem, vmem_limit_bytes=vmem_mib * MIB)


def _mod_row(t0):
    return jnp.where(t0 < N_CTX, 0, 1 + (t0 - N_CTX) // DEC_SEQ)


def _mod_spec(tm, chunk):
    return pl.BlockSpec((None, 1, D), lambda i: (_mod_row(i * tm), 0, chunk))


def _rms(x, g):
    return (x * lax.rsqrt(jnp.mean(x * x, axis=-1, keepdims=True) + EPS)) * g


def _ada_kernel(c_ref, w_ref, b_ref, o_ref):
    c = c_ref[...]
    s = c * jax.nn.sigmoid(c)
    o_ref[...] = jnp.dot(s.astype(BF16), w_ref[...].astype(BF16),
                         preferred_element_type=F32) + b_ref[...]


def _modulation(cvec, w_ada, b_ada):
    tn = 1024
    return pl.pallas_call(
        _ada_kernel,
        grid=(N_MOD * D // tn,),
        in_specs=[pl.BlockSpec((MOD_ROWS, D), lambda j: (0, 0)),
                  pl.BlockSpec((D, tn), lambda j: (0, j)),
                  pl.BlockSpec((1, tn), lambda j: (0, j))],
        out_specs=pl.BlockSpec((MOD_ROWS, tn), lambda j: (0, j)),
        out_shape=jax.ShapeDtypeStruct((MOD_ROWS, N_MOD * D), F32),
        compiler_params=_cparams(("arbitrary",), 40),
        name="ada_modulation",
    )(cvec, w_ada, b_ada.reshape(1, -1))


NORM_TM = 512


def _ctx_lat_specs(tm, width):
    ctx_tiles = N_CTX // tm
    return (pl.BlockSpec((tm, width), lambda i, *_: (jnp.minimum(i, ctx_tiles - 1), 0)),
            pl.BlockSpec((tm, width), lambda i, *_: (jnp.maximum(i - ctx_tiles, 0), 0)))


def _norm_mod_kernel(xc_ref, xl_ref, g_ref, sh_ref, sc_ref, o_ref):
    x = jnp.where(pl.program_id(0) * NORM_TM < N_CTX, xc_ref[...], xl_ref[...])
    y = _rms(x, g_ref[...])
    o_ref[...] = (y * (1.0 + sc_ref[...]) + sh_ref[...]).astype(BF16)


def _norm_mod(x_ctx, x_lat, g, mod3):
    tm = NORM_TM
    return pl.pallas_call(
        _norm_mod_kernel,
        grid=(N_TOK // tm,),
        in_specs=[*_ctx_lat_specs(tm, D),
                  pl.BlockSpec((1, D), lambda i: (0, 0)),
                  _mod_spec(tm, MOD_SH1),
                  _mod_spec(tm, MOD_SC1)],
        out_specs=pl.BlockSpec((tm, D), lambda i: (i, 0)),
        out_shape=jax.ShapeDtypeStruct((N_TOK, D), BF16),
        compiler_params=_cparams(("arbitrary",), 40),
        name="norm_modulate",
    )(x_ctx, x_lat, g.reshape(1, D), mod3, mod3)


def _proj_kernel(h_ref, w_ref, o_ref, wb_ref):
    @pl.when(pl.program_id(1) == 0)
    def _():
        wb_ref[...] = w_ref[...].astype(BF16)

    o_ref[...] = jnp.dot(h_ref[...], wb_ref[...], preferred_element_type=F32).astype(BF16)


def _in_proj(h, w_in):
    tm, tn = 1024, 1024
    return pl.pallas_call(
        _proj_kernel,
        grid=(IN_COLS // tn, N_TOK // tm),
        in_specs=[pl.BlockSpec((tm, D), lambda n, m: (m, 0)),
                  pl.BlockSpec((D, tn), lambda n, m: (0, n))],
        out_specs=pl.BlockSpec((tm, tn), lambda n, m: (m, n)),
        out_shape=jax.ShapeDtypeStruct((N_TOK, IN_COLS), BF16),
        scratch_shapes=[pltpu.VMEM((D, tn), BF16)],
        compiler_params=_cparams(("arbitrary", "arbitrary"), 48),
        name="in_proj",
    )(h, w_in)


CONV_TM = 256


def _conv_kernel(cx_ref, cb_ref, cc_ref, mg0_ref, mg1_ref, cw_ref, cbias_ref, w_ref, o_ref):
    i = pl.program_id(0)
    u = cc_ref[...].astype(F32) * cx_ref[...].astype(F32)
    seg = jnp.where(i * CONV_TM < N_CTX, SEQ, GRID_W)
    pos = lax.broadcasted_iota(I32, (CONV_TM, 1), 0) & (seg - 1)
    prev = jnp.where(pos == 0, 0.0, pltpu.roll(u, 1, 0))
    nxt = jnp.where(pos == seg - 1, 0.0, pltpu.roll(u, CONV_TM - 1, 0))
    cw = cw_ref[...]
    uc = prev * cw[0:1, :] + u * cw[1:2, :] + nxt * cw[2:3, :] + cbias_ref[...]
    z = (cb_ref[...].astype(F32) * uc).astype(BF16)
    y = jnp.dot(z, w_ref[...], preferred_element_type=F32)
    mg = jnp.concatenate([mg0_ref[...], mg1_ref[...]], axis=1).astype(F32)
    o_ref[...] = (jax.nn.sigmoid(mg) * y).astype(BF16)


def _conv_branch(proj, conv_w, conv_b, w_br_conv_bf):
    tm = CONV_TM
    cw = 1024

    def col(c):
        return pl.BlockSpec((tm, cw), lambda i: (i, c))

    return pl.pallas_call(
        _conv_kernel,
        grid=(N_TOK // tm,),
        in_specs=[col(COL_CX // cw), col(COL_CB // cw), col(COL_CC // cw),
                  col(COL_MGC // cw), col(COL_MGC // cw + 1),
                  pl.BlockSpec((3, CONV_W), lambda i: (0, 0)),
                  pl.BlockSpec((1, CONV_W), lambda i: (0, 0)),
                  pl.BlockSpec((CONV_W, D), lambda i: (0, 0))],
        out_specs=pl.BlockSpec((tm, D), lambda i: (i, 0)),
        out_shape=jax.ShapeDtypeStruct((N_TOK, D), BF16),
        compiler_params=_cparams(("arbitrary",), 48),
        name="conv_branch",
    )(proj, proj, proj, proj, proj, conv_w, conv_b.reshape(1, CONV_W), w_br_conv_bf)


def _log_sigmoid(x):
    return -(jnp.maximum(-x, 0.0) + jnp.log1p(jnp.exp(-jnp.abs(x))))


def _rope(x, cos, sin_signed):
    lane = lax.broadcasted_iota(I32, x.shape, 1)
    swapped = jnp.where((lane & 63) < 32, pltpu.roll(x, 96, 1), pltpu.roll(x, 32, 1))
    return x * cos + swapped * sin_signed


def _retention_kernel(*refs, seq_len, latent, hps):
    for hh in range(hps):
        _retention_head(refs, pl.program_id(1) * hps + hh, hh, seq_len=seq_len, latent=latent)


def _retention_head(refs, h, hh, *, seq_len, latent):
    if latent:
        (dl_ref, q_ref, k_ref, v_ref, g_ref, gn_ref, cos_ref, sin_ref, s0_ref,
         r_ref, o_acc) = refs
    else:
        dl_ref, q_ref, k_ref, v_ref, g_ref, gn_ref, r_ref, sout_ref, o_acc = refs
    c = CHUNK
    n_chunks = seq_len // c
    kcol = slice(hh * DK, (hh + 1) * DK)
    vcol = slice(hh * DV, (hh + 1) * DV)

    lgf = _log_sigmoid(jnp.full((c, c), dl_ref[0, h], F32))
    lgb = _log_sigmoid(jnp.full((c, c), dl_ref[1, h], F32))
    lgf_w = _log_sigmoid(jnp.full((c, DV), dl_ref[0, h], F32))
    lgb_w = _log_sigmoid(jnp.full((c, DV), dl_ref[1, h], F32))

    ii = lax.broadcasted_iota(I32, (c, c), 0).astype(F32)
    jj = lax.broadcasted_iota(I32, (c, c), 1).astype(F32)
    diff = ii - jj
    m_in = jnp.where(diff >= 0.0, jnp.exp(jnp.maximum(diff, 0.0) * lgf),
                     jnp.exp(jnp.maximum(-diff, 0.0) * lgb))
    io = lax.broadcasted_iota(I32, (c, DV), 0).astype(F32)
    dq_f = jnp.exp((io + 1.0) * lgf_w)
    dq_b = jnp.exp((c - io) * lgb_w)
    dk_f = jnp.exp((c - 1.0 - jj) * lgf)
    dk_b = jnp.exp(jj * lgb)
    dc_f = jnp.exp(c * lgf_w)
    dc_b = jnp.exp(c * lgb_w)

    q = q_ref[:, kcol].astype(F32)
    k = k_ref[:, kcol].astype(F32) * (DK ** -0.5)
    if latent:
        q = _rope(q, cos_ref[...], sin_ref[...])
        k = _rope(k, cos_ref[...], sin_ref[...])
        s_f = s0_ref[0, hh]
        s_b = s0_ref[1, hh]
    else:
        s_f = jnp.zeros((DK, DV), F32)
        s_b = jnp.zeros((DK, DV), F32)
    qb = q.astype(BF16)

    def chunk(x, n):
        return x[n * c:(n + 1) * c]

    for n in range(n_chunks):
        qn = chunk(qb, n)
        kt = chunk(k, n).T
        vn = v_ref[n * c:(n + 1) * c, vcol]
        s = jnp.dot(qn, kt.astype(BF16), preferred_element_type=F32)
        p = (s * m_in).astype(BF16)
        o = jnp.dot(p, vn, preferred_element_type=F32)
        o = o + jnp.dot(qn, s_f.astype(BF16), preferred_element_type=F32) * dq_f
        o_acc[n * c:(n + 1) * c, vcol] = o
        s_f = dc_f * s_f + jnp.dot((kt * dk_f).astype(BF16), vn, preferred_element_type=F32)

    for n in reversed(range(n_chunks)):
        qn = chunk(qb, n)
        kt = chunk(k, n).T
        vn = v_ref[n * c:(n + 1) * c, vcol]
        o_acc[n * c:(n + 1) * c, vcol] += (
            jnp.dot(qn, s_b.astype(BF16), preferred_element_type=F32) * dq_b)
        s_b = dc_b * s_b + jnp.dot((kt * dk_b).astype(BF16), vn, preferred_element_type=F32)

    if not latent:
        sout_ref[0, hh] = s_f
        sout_ref[1, hh] = s_b

    o = o_acc[:, vcol]
    mu = jnp.mean(o, axis=-1, keepdims=True)
    oc = o - mu
    var = jnp.mean(oc * oc, axis=-1, keepdims=True)
    on = oc * lax.rsqrt(var + EPS) * gn_ref[:, vcol]
    g = g_ref[:, vcol].astype(F32)
    r_ref[:, vcol] = ((g * jax.nn.sigmoid(g)) * on).astype(BF16)


def _retention(proj, decay_logit, gn_w, cos_t, sin_t, state0, *, latent):
    seq_len = DEC_SEQ if latent else SEQ
    nb = DEC_BATCH if latent else BATCH
    rb0 = (N_CTX // DEC_SEQ) if latent else 0

    hps = 1 if latent else 4
    assert COL_V % (DV * hps) == 0 and COL_G % (DV * hps) == 0 and COL_Q % (DK * hps) == 0

    def spec(width, col0):
        w = width * hps
        return pl.BlockSpec((seq_len, w), lambda b, h, dl: (rb0 + b, col0 // w + h))

    in_specs = [spec(DK, COL_Q), spec(DK, COL_K), spec(DV, COL_V), spec(DV, COL_G),
                pl.BlockSpec((1, DV * hps), lambda b, h, dl: (0, h))]
    args = [proj, proj, proj, proj, gn_w.reshape(1, HEADS * DV)]
    state_spec = pl.BlockSpec((None, None, 2, hps, DK, DV), lambda b, h, dl: (b, 0, 0, h, 0, 0))
    r_shape = jax.ShapeDtypeStruct((nb * seq_len, HEADS * DV), BF16)
    r_spec = pl.BlockSpec((seq_len, DV * hps), lambda b, h, dl: (b, h))
    if latent:
        in_specs += [pl.BlockSpec((seq_len, DK), lambda b, h, dl: (0, 0)),
                     pl.BlockSpec((seq_len, DK), lambda b, h, dl: (0, 0)),
                     state_spec]
        args += [cos_t, sin_t, state0]
        out_specs, out_shape = r_spec, r_shape
    else:
        out_specs = (r_spec, state_spec)
        out_shape = (r_shape, jax.ShapeDtypeStruct((BATCH, 1, 2, HEADS, DK, DV), F32))
    return pl.pallas_call(
        functools.partial(_retention_kernel, seq_len=seq_len, latent=latent, hps=hps),
        grid_spec=pltpu.PrefetchScalarGridSpec(
            num_scalar_prefetch=1, grid=(nb, HEADS // hps), in_specs=in_specs,
            out_specs=out_specs, scratch_shapes=[pltpu.VMEM((seq_len, DV * hps), F32)]),
        out_shape=out_shape,
        compiler_params=_cparams(("arbitrary", "arbitrary"), 40),
        name="retention_latent" if latent else "retention_context",
    )(decay_logit, *args)


MIX_TM = 256


def _mix_kernel(rc_ref, rl_ref, wbr_ref, a_ref, mg0_ref, mg1_ref, wout_ref, xc_ref, xl_ref, g1_ref,
                sc2_ref, sh2_ref, npost_ref, npre_ref, rwt_ref, rb_ref,
                x1_ref, h2_ref, ti_ref, tw_ref):
    is_ctx = pl.program_id(0) * MIX_TM < N_CTX
    r = jnp.where(is_ctx, rc_ref[...], rl_ref[...])
    x = jnp.where(is_ctx, xc_ref[...], xl_ref[...])
    y = jnp.dot(r, wbr_ref[...], preferred_element_type=F32)
    mg = jnp.concatenate([mg0_ref[...], mg1_ref[...]], axis=1).astype(F32)
    mix = a_ref[...].astype(F32) + jax.nn.sigmoid(mg) * y
    out = jnp.dot(mix.astype(BF16), wout_ref[...], preferred_element_type=F32)
    x1 = x + g1_ref[...] * _rms(out, npost_ref[...])
    x1_ref[...] = x1
    h2 = _rms(x1, npre_ref[...]) * (1.0 + sc2_ref[...]) + sh2_ref[...]
    h2_ref[...] = h2
    logits = lax.dot_general(rwt_ref[...], h2.astype(BF16), (((1,), (1,)), ((), ())),
                             preferred_element_type=F32) + rb_ref[...]
    eidx = lax.broadcasted_iota(I32, logits.shape, 0).astype(F32)
    vals, idxs = [], []
    for _ in range(TOP_K):
        m = jnp.max(logits, axis=0, keepdims=True)
        am = jnp.min(jnp.where(logits == m, eidx, float(N_EXP)), axis=0, keepdims=True)
        vals.append(m)
        idxs.append(am)
        logits = jnp.where(eidx == am, -jnp.inf, logits)
    es = [jnp.exp(v - vals[0]) for v in vals]
    denom = es[0] + es[1] + es[2] + es[3]
    for kk in range(TOP_K):
        ti_ref[kk:kk + 1, :] = idxs[kk].astype(I32)
        tw_ref[kk:kk + 1, :] = es[kk] / denom


def _mix_route(r_ctx, r_lat, wbr_bf, a, proj, wout_bf, x_ctx, x_lat, mod3, npost, npre, rwt_bf, rb):
    tm = MIX_TM
    cw = 1024

    def full(shape):
        return pl.BlockSpec(shape, lambda i: (0,) * len(shape), pipeline_mode=pl.Buffered(1))

    def row(width):
        return pl.BlockSpec((tm, width), lambda i: (i, 0))

    return pl.pallas_call(
        _mix_kernel,
        grid=(N_TOK // tm,),
        in_specs=[*_ctx_lat_specs(tm, D),
                  full((D, D)), row(D),
                  pl.BlockSpec((tm, cw), lambda i: (i, COL_MGR // cw)),
                  pl.BlockSpec((tm, cw), lambda i: (i, COL_MGR // cw + 1)),
                  full((D, D)), *_ctx_lat_specs(tm, D),
                  _mod_spec(tm, MOD_G1), _mod_spec(tm, MOD_SC2), _mod_spec(tm, MOD_SH2),
                  full((1, D)), full((1, D)), full((N_EXP, D)), full((N_EXP, 1))],
        out_specs=(row(D), row(D),
                   pl.BlockSpec((TOP_K, tm), lambda i: (0, i)),
                   pl.BlockSpec((TOP_K, tm), lambda i: (0, i))),
        out_shape=(jax.ShapeDtypeStruct((N_TOK, D), F32),
                   jax.ShapeDtypeStruct((N_TOK, D), F32),
                   jax.ShapeDtypeStruct((TOP_K, N_TOK), I32),
                   jax.ShapeDtypeStruct((TOP_K, N_TOK), F32)),
        compiler_params=_cparams(("arbitrary",), 56),
        name="mix_out_route",
    )(r_ctx, r_lat, wbr_bf, a, proj, proj, wout_bf, x_ctx, x_lat, mod3, mod3, mod3,
      npost.reshape(1, D), npre.reshape(1, D), rwt_bf, rb.reshape(N_EXP, 1))


def _routing_tables(top_i):
    e = top_i.reshape(-1)
    onehot = e[:, None] == jnp.arange(N_EXP, dtype=I32)[None, :]
    blk = 256
    oh = onehot.astype(F32).reshape(N_PAIR // blk, blk, N_EXP)
    tri = (jnp.arange(blk)[:, None] >= jnp.arange(blk)[None, :]).astype(F32)
    within = jnp.einsum("ij,bjk->bik", tri, oh)
    blk_tot = within[:, -1, :]
    blk_off = jnp.cumsum(blk_tot, axis=0) - blk_tot
    csum = (within + blk_off[:, None, :]).reshape(N_PAIR, N_EXP)
    rank = jnp.sum(jnp.where(onehot, csum, 0.0), axis=1).astype(I32) - 1
    counts = csum[-1].astype(I32)
    tiles_e = (counts + MOE_TM - 1) // MOE_TM
    tile_end = jnp.cumsum(tiles_e)
    tile_start = tile_end - tiles_e
    slot = tile_start[e] * MOE_TM + rank
    n_tiles = tile_end[-1]
    tok = jnp.tile(jnp.arange(N_TOK, dtype=I32), TOP_K)
    tok_of_slot = jnp.zeros((MOE_ROWS,), I32).at[slot].set(tok, unique_indices=True)
    tid = jnp.arange(MOE_TILES, dtype=I32)
    tile_exp = jnp.minimum(jnp.sum((tid[:, None] >= tile_end[None, :]).astype(I32), axis=1),
                           N_EXP - 1)
    tile_rows = jnp.clip(counts[tile_exp] - MOE_TM * (tid - tile_start[tile_exp]), 0, MOE_TM)
    tile_rows = jnp.where(tid < n_tiles, tile_rows, 0).astype(I32)

    groups_e = (tiles_e + MOE_SUB - 1) // MOE_SUB
    g_end = jnp.cumsum(groups_e)
    g_start = g_end - groups_e
    n_groups = g_end[-1]
    gid = jnp.arange(MOE_GROUPS, dtype=I32)
    used = gid < n_groups
    g_exp = jnp.minimum(jnp.sum((gid[:, None] >= g_end[None, :]).astype(I32), axis=1), N_EXP - 1)
    g_exp = jnp.where(used, g_exp, g_exp[jnp.maximum(n_groups - 1, 0)])
    g_in_e = gid - g_start[g_exp]
    first = tile_start[g_exp] + MOE_SUB * g_in_e
    n_valid = jnp.where(used, jnp.clip(tiles_e[g_exp] - MOE_SUB * g_in_e, 0, MOE_SUB), 0)
    sub = jnp.arange(MOE_SUB, dtype=I32)[None, :]
    x_tile = first[:, None] + jnp.minimum(sub, jnp.maximum(n_valid[:, None] - 1, 0))
    x_tile = jnp.where(used[:, None], x_tile, x_tile[jnp.maximum(n_groups - 1, 0)][None, :])
    tail = n_tiles + (gid[:, None] - n_groups) * MOE_SUB + sub
    o_tile = jnp.where(used[:, None],
                       jnp.where(sub < n_valid[:, None], first[:, None] + sub, -1),
                       jnp.where(tail < MOE_TILES, tail, -1))
    n_steps = (n_groups + (MOE_TILES - n_tiles + MOE_SUB - 1) // MOE_SUB).astype(I32)
    return (slot, tok_of_slot, tile_rows, n_tiles.reshape(1).astype(I32),
            x_tile.reshape(-1).astype(I32),
            o_tile.reshape(-1).astype(I32), n_valid.astype(I32), g_exp.astype(I32), n_steps)


def _row_copy(src_hbm, row, dst_ref, r, sem):
    return pltpu.make_async_copy(src_hbm.at[pl.ds(row, 1), :], dst_ref.at[pl.ds(r, 1), :], sem)


DISP_RING = 8
DISP_LAG = DISP_RING - 1
DISP_BLK = 8


def _dispatch_kernel(nt_ref, rows_ref, tos_ref, x_hbm, o_ref, buf, sem):
    j = pl.program_id(0)
    nt = nt_ref[0]
    done = j - DISP_LAG

    def row_blocks(t):
        return (rows_ref[t] + DISP_BLK - 1) // DISP_BLK

    @pl.when(j < nt)
    def _():
        b = j & (DISP_RING - 1)

        @pl.when(rows_ref[j] < MOE_TM)
        def _():
            buf[b] = jnp.zeros((MOE_TM, D), F32)

        def issue(blk, carry):
            for p in range(DISP_BLK):
                r = blk * DISP_BLK + p
                _row_copy(x_hbm, tos_ref[0, r], buf.at[b], r, sem.at[b]).start()
            return carry

        lax.fori_loop(0, row_blocks(j), issue, 0)

    @pl.when(jnp.logical_and(done >= 0, done < nt))
    def _():
        b = done & (DISP_RING - 1)

        def wait(blk, carry):
            for p in range(DISP_BLK):
                _row_copy(x_hbm, 0, buf.at[b], blk * DISP_BLK + p, sem.at[b]).wait()
            return carry

        lax.fori_loop(0, row_blocks(done), wait, 0)
        o_ref[...] = buf[b].astype(BF16)

    @pl.when(jnp.logical_and(done >= 0, done >= nt))
    def _():
        o_ref[...] = jnp.zeros_like(o_ref)


def _dispatch(h2, tok_of_slot, n_tiles, tile_rows):
    return pl.pallas_call(
        _dispatch_kernel,
        grid_spec=pltpu.PrefetchScalarGridSpec(
            num_scalar_prefetch=2, grid=(XS_TILES + DISP_LAG,),
            in_specs=[pl.BlockSpec((None, 1, MOE_TM),
                                   lambda j, nt, rows: (jnp.minimum(j, MOE_TILES - 1), 0, 0),
                                   memory_space=pltpu.SMEM),
                      pl.BlockSpec(memory_space=pl.ANY)],
            out_specs=pl.BlockSpec((MOE_TM, D),
                                   lambda j, nt, rows: (jnp.maximum(j - DISP_LAG, 0), 0)),
            scratch_shapes=[pltpu.VMEM((DISP_RING, MOE_TM, D), F32),
                            pltpu.SemaphoreType.DMA((DISP_RING,))]),
        out_shape=jax.ShapeDtypeStruct((XS_TILES * MOE_TM, D), BF16),
        compiler_params=_cparams(("arbitrary",), 32),
        name="moe_dispatch",
    )(n_tiles, tile_rows, tok_of_slot.reshape(MOE_TILES, 1, MOE_TM), h2)


def _tile_store(acc_ref, b, i, y_hbm, tile, sem):
    return pltpu.make_async_copy(acc_ref.at[b, i], y_hbm.at[pl.ds(tile * MOE_TM, MOE_TM), :],
                                 sem.at[b])


def _moe_kernel(xt_ref, ot_ref, nv_ref, ge_ref, *refs):
    (x_ref, wg_ref, wu_ref, wd_ref, bg_ref, bu_ref, bd_ref, y_hbm,
     wgb_ref, wub_ref, wdb_ref, acc_ref, sem) = refs
    g = pl.program_id(0)
    c = pl.program_id(1)
    nv = nv_ref[g]
    b = g & 1
    last_chunk = c == MOE_NCH - 1

    def wait_group(gw):
        for i in range(MOE_SUB):
            @pl.when(ot_ref[gw * MOE_SUB + i] >= 0)
            def _(i=i):
                _tile_store(acc_ref, gw & 1, i, y_hbm, 0, sem).wait()

    @pl.when(jnp.logical_and(c == 0, g >= 2))
    def _():
        wait_group(g - 2)

    @pl.when(jnp.logical_and(c == 0, nv > 0))
    def _():
        for i in range(MOE_SUB):
            acc_ref[b, i] = jnp.broadcast_to(bd_ref[...], (MOE_TM, D))

    def group_step(m):
        wgb_ref[...] = wg_ref[...].astype(BF16)
        wub_ref[...] = wu_ref[...].astype(BF16)
        wdb_ref[...] = wd_ref[...].astype(BF16)
        x = x_ref[:m * MOE_TM]
        gate = jnp.dot(x, wgb_ref[...], preferred_element_type=F32) + bg_ref[...]
        up = jnp.dot(x, wub_ref[...], preferred_element_type=F32) + bu_ref[...]
        gate = jnp.minimum(gate, LIMIT)
        up = jnp.clip(up, -LIMIT, LIMIT)
        act = (gate * jax.nn.sigmoid(ALPHA * gate) * (up + 1.0)).astype(BF16)
        for i in range(m):
            acc_ref[b, i] += jnp.dot(act[i * MOE_TM:(i + 1) * MOE_TM], wdb_ref[...],
                                     preferred_element_type=F32)

    for m in range(1, MOE_SUB + 1):
        pl.when(nv == m)(functools.partial(group_step, m))

    for i in range(MOE_SUB):
        @pl.when(jnp.logical_and(jnp.logical_and(i >= nv, last_chunk), ot_ref[g * MOE_SUB + i] >= 0))
        def _(i=i):
            acc_ref[b, i] = jnp.zeros((MOE_TM, D), F32)

        @pl.when(jnp.logical_and(last_chunk, ot_ref[g * MOE_SUB + i] >= 0))
        def _(i=i):
            _tile_store(acc_ref, b, i, y_hbm, ot_ref[g * MOE_SUB + i], sem).start()

    @pl.when(jnp.logical_and(last_chunk, g == pl.num_programs(0) - 1))
    def _():
        @pl.when(g >= 1)
        def _():
            wait_group(g - 1)

        wait_group(g)


def _moe(xs, wg, wu, wd, bg, bu, bd, x_tile, o_tile, n_valid, g_exp, n_steps):
    def chunk_of(g, c, nv):
        return jnp.where(nv[g] > 0, c, MOE_NCH - 1)

    x_spec = pl.BlockSpec((pl.Element(MOE_SUB * MOE_TM), pl.Element(D)),
                          lambda g, c, xt, ot, nv, ge: (xt[g * MOE_SUB] * MOE_TM, 0))

    w_in_spec = pl.BlockSpec((None, D, MOE_CH),
                             lambda g, c, xt, ot, nv, ge: (ge[g], 0, chunk_of(g, c, nv)))
    b_in_spec = pl.BlockSpec((None, 1, MOE_CH),
                             lambda g, c, xt, ot, nv, ge: (ge[g], 0, chunk_of(g, c, nv)))
    w_out_spec = pl.BlockSpec((None, MOE_CH, D),
                              lambda g, c, xt, ot, nv, ge: (ge[g], chunk_of(g, c, nv), 0))
    b_out_spec = pl.BlockSpec((None, 1, D), lambda g, c, xt, ot, nv, ge: (ge[g], 0, 0))
    return pl.pallas_call(
        _moe_kernel,
        grid_spec=pltpu.PrefetchScalarGridSpec(
            num_scalar_prefetch=4, grid=(n_steps, MOE_NCH),
            in_specs=[x_spec, w_in_spec, w_in_spec, w_out_spec, b_in_spec, b_in_spec, b_out_spec],
            out_specs=pl.BlockSpec(memory_space=pl.ANY),
            scratch_shapes=[pltpu.VMEM((D, MOE_CH), BF16), pltpu.VMEM((D, MOE_CH), BF16),
                            pltpu.VMEM((MOE_CH, D), BF16),
                            pltpu.VMEM((2, MOE_SUB, MOE_TM, D), F32),
                            pltpu.SemaphoreType.DMA((2,))]),
        out_shape=jax.ShapeDtypeStruct((MOE_ROWS, D), F32),
        compiler_params=_cparams(("arbitrary", "arbitrary"), 62),
        name="moe_experts",
    )(x_tile, o_tile, n_valid, g_exp, xs, wg, wu, wd,
      bg.reshape(N_EXP, 1, D_FF), bu.reshape(N_EXP, 1, D_FF), bd.reshape(N_EXP, 1, D))


COMB_TM = 256


COMB_TILES = N_TOK // COMB_TM
COMB_BLK = 8


def _combine_kernel(slot_ref, ys_hbm, w_ref, x1_ref, g2_ref, npost_ref, oc_ref, ol_ref, buf, sem):
    i = pl.program_id(0)

    def start_tile(b):
        def issue(blk, carry):
            for kk in range(TOP_K):
                for p in range(COMB_BLK):
                    r = blk * COMB_BLK + p
                    _row_copy(ys_hbm, slot_ref[0, kk * COMB_TM + r], buf.at[b, kk], r,
                              sem.at[b]).start()
            return carry

        lax.fori_loop(0, COMB_TM // COMB_BLK, issue, 0)

    def wait_tile(b):
        def wait(blk, carry):
            for kk in range(TOP_K):
                for p in range(COMB_BLK):
                    _row_copy(ys_hbm, 0, buf.at[b, kk], blk * COMB_BLK + p, sem.at[b]).wait()
            return carry

        lax.fori_loop(0, COMB_TM // COMB_BLK, wait, 0)

    for par in range(2):
        @pl.when(jnp.logical_and(i < COMB_TILES, (i & 1) == par))
        def _(par=par):
            start_tile(par)

    for par in range(2):
        @pl.when(jnp.logical_and(i >= 1, ((i - 1) & 1) == par))
        def _(par=par):
            wait_tile(par)

    @pl.when(i >= 1)
    def _():
        b = (i - 1) & 1
        w = w_ref[...]
        f = buf[b, 0] * w[:, 0:1]
        for kk in range(1, TOP_K):
            f = f + buf[b, kk] * w[:, kk:kk + 1]
        out = x1_ref[...] + g2_ref[...] * _rms(f, npost_ref[...])
        base = (i - 1) * COMB_TM

        @pl.when(base < N_CTX)
        def _():
            oc_ref[...] = out

        @pl.when(base >= N_CTX)
        def _():
            ol_ref[...] = out


def _combine(ys, slot, top_w_t, x1, mod3, npost):
    tm = COMB_TM

    def prev(i):
        return jnp.maximum(i - 1, 0)

    oc_spec, ol_spec = _ctx_lat_specs(tm, D)
    return pl.pallas_call(
        _combine_kernel,
        grid=(COMB_TILES + 1,),
        in_specs=[pl.BlockSpec((None, 1, TOP_K * tm),
                               lambda i: (jnp.minimum(i, COMB_TILES - 1), 0, 0),
                               memory_space=pltpu.SMEM),
                  pl.BlockSpec(memory_space=pl.ANY),
                  pl.BlockSpec((tm, TOP_K), lambda i: (prev(i), 0)),
                  pl.BlockSpec((tm, D), lambda i: (prev(i), 0)),
                  pl.BlockSpec((None, 1, D), lambda i: (_mod_row(prev(i) * tm), 0, MOD_G2)),
                  pl.BlockSpec((1, D), lambda i: (0, 0))],
        out_specs=(pl.BlockSpec((tm, D), lambda i: oc_spec.index_map(prev(i))),
                   pl.BlockSpec((tm, D), lambda i: ol_spec.index_map(prev(i)))),
        scratch_shapes=[pltpu.VMEM((2, TOP_K, tm, D), F32), pltpu.SemaphoreType.DMA((2,))],
        out_shape=(jax.ShapeDtypeStruct((N_CTX, D), F32), jax.ShapeDtypeStruct((N_LAT, D), F32)),
        compiler_params=_cparams(("arbitrary",), 48),
        name="moe_combine",
    )(slot.reshape(TOP_K, COMB_TILES, tm).transpose(1, 0, 2).reshape(COMB_TILES, 1, TOP_K * tm),
      ys, top_w_t, x1, mod3, npost.reshape(1, D))


def _rope_tables():
    pos = jnp.arange(DEC_SEQ)
    row = (pos // GRID_W).astype(F32)
    col = (pos % GRID_W).astype(F32)
    n_freq = DK // 4
    freqs = THETA ** (-jnp.arange(n_freq, dtype=F32) / n_freq)
    ar = row[:, None] * freqs
    ac = col[:, None] * freqs
    cos_t = jnp.concatenate([jnp.cos(ar), jnp.cos(ar), jnp.cos(ac), jnp.cos(ac)], axis=1)
    sin_t = jnp.concatenate([-jnp.sin(ar), jnp.sin(ar), -jnp.sin(ac), jnp.sin(ac)], axis=1)
    return cos_t.astype(F32), sin_t.astype(F32)


def kernel(x_prompt, x_sample, state_ret, c, c_ctx, w_ada, b_ada, norm_mix_pre, norm_mix_post,
           norm_ffn_pre, norm_ffn_post, w_in, conv_w, conv_b, ret_decay_logit, ret_gn_w,
           w_br_conv, w_br_ret, w_out, router_w, router_b, moe_w_gate, moe_b_gate,
           moe_w_up, moe_b_up, moe_w_down, moe_b_down):
    layer = 0
    x_ctx = x_prompt.reshape(N_CTX, D)
    x_lat = x_sample.reshape(N_LAT, D)
    cvec = jnp.concatenate(
        [c_ctx[None, :], c, jnp.zeros((MOD_ROWS - 1 - DEC_BATCH, D), F32)], axis=0)

    mod = _modulation(cvec, w_ada[layer], b_ada[layer])
    mod3 = mod.reshape(MOD_ROWS, 1, N_MOD * D)

    h1 = _norm_mod(x_ctx, x_lat, norm_mix_pre[layer], mod3)
    proj = _in_proj(h1, w_in[layer])

    a = _conv_branch(proj, conv_w[layer], conv_b[layer], w_br_conv[layer].astype(BF16))

    cos_t, sin_t = _rope_tables()
    dl = ret_decay_logit[layer].astype(F32)
    r_ctx, new_state = _retention(proj, dl, ret_gn_w[layer], None, None, None, latent=False)
    r_lat = _retention(proj, dl, ret_gn_w[layer], cos_t, sin_t, state_ret, latent=True)

    x1, h2, top_i, top_w = _mix_route(
        r_ctx, r_lat, w_br_ret[layer].astype(BF16), a, proj, w_out[layer].astype(BF16),
        x_ctx, x_lat, mod3, norm_mix_post[layer], norm_ffn_pre[layer],
        router_w[layer].T.astype(BF16), router_b[layer])

    (slot, tok_of_slot, tile_rows, n_tiles, x_tile, o_tile, n_valid, g_exp,
     n_steps) = _routing_tables(top_i)
    xs = _dispatch(h2, tok_of_slot, n_tiles, tile_rows)
    ys = _moe(xs, moe_w_gate[layer], moe_w_up[layer], moe_w_down[layer],
              moe_b_gate[layer], moe_b_up[layer], moe_b_down[layer],
              x_tile, o_tile, n_valid, g_exp, n_steps)
    y_ctx, y_lat = _combine(ys, slot, top_w.T, x1, mod3, norm_ffn_post[layer])

    return (y_ctx.reshape(BATCH, SEQ, D), y_lat.reshape(DEC_BATCH, DEC_SEQ, D), new_state)
```

```python
import functools

import jax
import jax.numpy as jnp
import numpy as np
from jax import lax
from jax.experimental import pallas as pl
from jax.experimental.pallas import tpu as pltpu

F32 = jnp.float32
BF16 = jnp.bfloat16
I32 = jnp.int32

D = 2048
BATCH = 16
SEQ = 256
DEC_BATCH = 2
DEC_SEQ = 1024
GRID_W = 64
CONV_W = 1024
HEADS = 8
DK = 128
DV = 256
CHUNK = 128
N_EXP = 32
TOP_K = 4
D_FF = 2048
LIMIT = 7.0
ALPHA = 1.702
THETA = 10000.0
EPS = 1e-6
N_MOD = 6
N_CTX = BATCH * SEQ
N_LAT = DEC_BATCH * DEC_SEQ
N_TOK = N_CTX + N_LAT
IN_COLS = 3 * CONV_W + 2 * HEADS * DK + 2 * HEADS * DV + 2 * D
COL_CX, COL_CB, COL_CC = 0, CONV_W, 2 * CONV_W
COL_Q = 3 * CONV_W
COL_K = COL_Q + HEADS * DK
COL_V = COL_K + HEADS * DK
COL_G = COL_V + HEADS * DV
COL_MGC = COL_G + HEADS * DV
COL_MGR = COL_MGC + D
MOD_SH1, MOD_SC1, MOD_G1, MOD_SH2, MOD_SC2, MOD_G2 = range(6)
MOD_ROWS = 8

MOE_TM = 256
N_PAIR = N_TOK * TOP_K
MOE_TILES = (N_PAIR + N_EXP * (MOE_TM - 1)) // MOE_TM + 1
MOE_ROWS = MOE_TILES * MOE_TM
MOE_SUB = 4
MOE_GROUPS = (MOE_TILES + (MOE_SUB - 1) * N_EXP) // MOE_SUB
MOE_CH = 512
MOE_NCH = D_FF // MOE_CH
XS_TILES = MOE_TILES + MOE_SUB - 1

MIB = 1024 * 1024


def _cparams(sem, vmem_mib):
    return pltpu.CompilerParams(dimension_semantics=sem, vmem_limit_bytes=vmem_mib * MIB)


def _mod_row(t0):
    return jnp.where(t0 < N_CTX, 0, 1 + (t0 - N_CTX) // DEC_SEQ)


def _mod_spec(tm, chunk):
    return pl.BlockSpec((None, 1, D), lambda i: (_mod_row(i * tm), 0, chunk))


def _rms(x, g):
    return (x * lax.rsqrt(jnp.mean(x * x, axis=-1, keepdims=True) + EPS)) * g


def _ada_kernel(c_ref, w_ref, b_ref, o_ref):
    c = c_ref[...]
    s = c * jax.nn.sigmoid(c)
    o_ref[...] = jnp.dot(s.astype(BF16), w_ref[...].astype(BF16),
                         preferred_element_type=F32) + b_ref[...]


def _modulation(cvec, w_ada, b_ada):
    tn = 1024
    return pl.pallas_call(
        _ada_kernel,
        grid=(N_MOD * D // tn,),
        in_specs=[pl.BlockSpec((MOD_ROWS, D), lambda j: (0, 0)),
                  pl.BlockSpec((D, tn), lambda j: (0, j)),
                  pl.BlockSpec((1, tn), lambda j: (0, j))],
        out_specs=pl.BlockSpec((MOD_ROWS, tn), lambda j: (0, j)),
        out_shape=jax.ShapeDtypeStruct((MOD_ROWS, N_MOD * D), F32),
        compiler_params=_cparams(("arbitrary",), 40),
        name="ada_modulation",
    )(cvec, w_ada, b_ada.reshape(1, -1))


NORM_TM = 512


def _ctx_lat_specs(tm, width):
    ctx_tiles = N_CTX // tm
    return (pl.BlockSpec((tm, width), lambda i, *_: (jnp.minimum(i, ctx_tiles - 1), 0)),
            pl.BlockSpec((tm, width), lambda i, *_: (jnp.maximum(i - ctx_tiles, 0), 0)))


def _norm_mod_kernel(xc_ref, xl_ref, g_ref, sh_ref, sc_ref, o_ref):
    x = jnp.where(pl.program_id(0) * NORM_TM < N_CTX, xc_ref[...], xl_ref[...])
    y = _rms(x, g_ref[...])
    o_ref[...] = (y * (1.0 + sc_ref[...]) + sh_ref[...]).astype(BF16)


def _norm_mod(x_ctx, x_lat, g, mod3):
    tm = NORM_TM
    return pl.pallas_call(
        _norm_mod_kernel,
        grid=(N_TOK // tm,),
        in_specs=[*_ctx_lat_specs(tm, D),
                  pl.BlockSpec((1, D), lambda i: (0, 0)),
                  _mod_spec(tm, MOD_SH1),
                  _mod_spec(tm, MOD_SC1)],
        out_specs=pl.BlockSpec((tm, D), lambda i: (i, 0)),
        out_shape=jax.ShapeDtypeStruct((N_TOK, D), BF16),
        compiler_params=_cparams(("arbitrary",), 40),
        name="norm_modulate",
    )(x_ctx, x_lat, g.reshape(1, D), mod3, mod3)


def _proj_kernel(h_ref, w_ref, o_ref, wb_ref):
    @pl.when(pl.program_id(1) == 0)
    def _():
        wb_ref[...] = w_ref[...].astype(BF16)

    o_ref[...] = jnp.dot(h_ref[...], wb_ref[...], preferred_element_type=F32).astype(BF16)


def _in_proj(h, w_in):
    tm, tn = 2048, 1024
    return pl.pallas_call(
        _proj_kernel,
        grid=(IN_COLS // tn, N_TOK // tm),
        in_specs=[pl.BlockSpec((tm, D), lambda n, m: (m, 0)),
                  pl.BlockSpec((D, tn), lambda n, m: (0, n))],
        out_specs=pl.BlockSpec((tm, tn), lambda n, m: (m, n)),
        out_shape=jax.ShapeDtypeStruct((N_TOK, IN_COLS), BF16),
        scratch_shapes=[pltpu.VMEM((D, tn), BF16)],
        compiler_params=_cparams(("arbitrary", "arbitrary"), 60),
        name="in_proj",
    )(h, w_in)


CONV_TM = 256


def _conv_kernel(cx_ref, cb_ref, cc_ref, mg0_ref, mg1_ref, cw_ref, cbias_ref, w_ref, o_ref):
    i = pl.program_id(0)
    u = cc_ref[...].astype(F32) * cx_ref[...].astype(F32)
    seg = jnp.where(i * CONV_TM < N_CTX, SEQ, GRID_W)
    pos = lax.broadcasted_iota(I32, (CONV_TM, 1), 0) & (seg - 1)
    prev = jnp.where(pos == 0, 0.0, pltpu.roll(u, 1, 0))
    nxt = jnp.where(pos == seg - 1, 0.0, pltpu.roll(u, CONV_TM - 1, 0))
    cw = cw_ref[...]
    uc = prev * cw[0:1, :] + u * cw[1:2, :] + nxt * cw[2:3, :] + cbias_ref[...]
    z = (cb_ref[...].astype(F32) * uc).astype(BF16)
    y = jnp.dot(z, w_ref[...], preferred_element_type=F32)
    mg = jnp.concatenate([mg0_ref[...], mg1_ref[...]], axis=1).astype(F32)
    o_ref[...] = (jax.nn.sigmoid(mg) * y).astype(BF16)


def _conv_branch(proj, conv_w, conv_b, w_br_conv_bf):
    tm = CONV_TM
    cw = 1024

    def col(c):
        return pl.BlockSpec((tm, cw), lambda i: (i, c))

    return pl.pallas_call(
        _conv_kernel,
        grid=(N_TOK // tm,),
        in_specs=[col(COL_CX // cw), col(COL_CB // cw), col(COL_CC // cw),
                  col(COL_MGC // cw), col(COL_MGC // cw + 1),
                  pl.BlockSpec((3, CONV_W), lambda i: (0, 0)),
                  pl.BlockSpec((1, CONV_W), lambda i: (0, 0)),
                  pl.BlockSpec((CONV_W, D), lambda i: (0, 0))],
        out_specs=pl.BlockSpec((tm, D), lambda i: (i, 0)),
        out_shape=jax.ShapeDtypeStruct((N_TOK, D), BF16),
        compiler_params=_cparams(("arbitrary",), 48),
        name="conv_branch",
    )(proj, proj, proj, proj, proj, conv_w, conv_b.reshape(1, CONV_W), w_br_conv_bf)


def _log_sigmoid(x):
    return -(jnp.maximum(-x, 0.0) + jnp.log1p(jnp.exp(-jnp.abs(x))))


def _rope(x, cos, sin_signed):
    lane = lax.broadcasted_iota(I32, x.shape, 1)
    swapped = jnp.where((lane & 63) < 32, pltpu.roll(x, 96, 1), pltpu.roll(x, 32, 1))
    return x * cos + swapped * sin_signed


def _retention_kernel(*refs, seq_len, latent, hps):
    for hh in range(hps):
        _retention_head(refs, pl.program_id(1) * hps + hh, hh, seq_len=seq_len, latent=latent)


def _retention_head(refs, h, hh, *, seq_len, latent):
    if latent:
        (dl_ref, q_ref, k_ref, v_ref, g_ref, gn_ref, cos_ref, sin_ref, s0_ref,
         r_ref, o_acc) = refs
    else:
        dl_ref, q_ref, k_ref, v_ref, g_ref, gn_ref, r_ref, sout_ref, o_acc = refs
    c = CHUNK
    n_chunks = seq_len // c
    kcol = slice(hh * DK, (hh + 1) * DK)
    vcol = slice(hh * DV, (hh + 1) * DV)

    lgf = _log_sigmoid(jnp.full((c, c), dl_ref[0, h], F32))
    lgb = _log_sigmoid(jnp.full((c, c), dl_ref[1, h], F32))
    lgf_w = _log_sigmoid(jnp.full((c, DV), dl_ref[0, h], F32))
    lgb_w = _log_sigmoid(jnp.full((c, DV), dl_ref[1, h], F32))

    ii = lax.broadcasted_iota(I32, (c, c), 0).astype(F32)
    jj = lax.broadcasted_iota(I32, (c, c), 1).astype(F32)
    diff = ii - jj
    m_in = jnp.where(diff >= 0.0, jnp.exp(jnp.maximum(diff, 0.0) * lgf),
                     jnp.exp(jnp.maximum(-diff, 0.0) * lgb))
    io = lax.broadcasted_iota(I32, (c, DV), 0).astype(F32)
    dq_f = jnp.exp((io + 1.0) * lgf_w)
    dq_b = jnp.exp((c - io) * lgb_w)
    dk_f = jnp.exp((c - 1.0 - jj) * lgf)
    dk_b = jnp.exp(jj * lgb)
    dc_f = jnp.exp(c * lgf_w)
    dc_b = jnp.exp(c * lgb_w)

    q = q_ref[:, kcol].astype(F32)
    k = k_ref[:, kcol].astype(F32) * (DK ** -0.5)
    if latent:
        q = _rope(q, cos_ref[...], sin_ref[...])
        k = _rope(k, cos_ref[...], sin_ref[...])
        s_f = s0_ref[0, hh]
        s_b = s0_ref[1, hh]
    else:
        s_f = jnp.zeros((DK, DV), F32)
        s_b = jnp.zeros((DK, DV), F32)
    qb = q.astype(BF16)

    def chunk(x, n):
        return x[n * c:(n + 1) * c]

    for n in range(n_chunks):
        qn = chunk(qb, n)
        kt = chunk(k, n).T
        vn = v_ref[n * c:(n + 1) * c, vcol]
        s = jnp.dot(qn, kt.astype(BF16), preferred_element_type=F32)
        p = (s * m_in).astype(BF16)
        o = jnp.dot(p, vn, preferred_element_type=F32)
        o = o + jnp.dot(qn, s_f.astype(BF16), preferred_element_type=F32) * dq_f
        o_acc[n * c:(n + 1) * c, vcol] = o
        s_f = dc_f * s_f + jnp.dot((kt * dk_f).astype(BF16), vn, preferred_element_type=F32)

    for n in reversed(range(n_chunks)):
        qn = chunk(qb, n)
        kt = chunk(k, n).T
        vn = v_ref[n * c:(n + 1) * c, vcol]
        o_acc[n * c:(n + 1) * c, vcol] += (
            jnp.dot(qn, s_b.astype(BF16), preferred_element_type=F32) * dq_b)
        s_b = dc_b * s_b + jnp.dot((kt * dk_b).astype(BF16), vn, preferred_element_type=F32)

    if not latent:
        sout_ref[0, hh] = s_f
        sout_ref[1, hh] = s_b

    o = o_acc[:, vcol]
    mu = jnp.mean(o, axis=-1, keepdims=True)
    oc = o - mu
    var = jnp.mean(oc * oc, axis=-1, keepdims=True)
    on = oc * lax.rsqrt(var + EPS) * gn_ref[:, vcol]
    g = g_ref[:, vcol].astype(F32)
    r_ref[:, vcol] = ((g * jax.nn.sigmoid(g)) * on).astype(BF16)


def _retention(proj, decay_logit, gn_w, cos_t, sin_t, state0, *, latent):
    seq_len = DEC_SEQ if latent else SEQ
    nb = DEC_BATCH if latent else BATCH
    rb0 = (N_CTX // DEC_SEQ) if latent else 0

    hps = 1 if latent else 4
    assert COL_V % (DV * hps) == 0 and COL_G % (DV * hps) == 0 and COL_Q % (DK * hps) == 0

    def spec(width, col0):
        w = width * hps
        return pl.BlockSpec((seq_len, w), lambda b, h, dl: (rb0 + b, col0 // w + h))

    in_specs = [spec(DK, COL_Q), spec(DK, COL_K), spec(DV, COL_V), spec(DV, COL_G),
                pl.BlockSpec((1, DV * hps), lambda b, h, dl: (0, h))]
    args = [proj, proj, proj, proj, gn_w.reshape(1, HEADS * DV)]
    state_spec = pl.BlockSpec((None, None, 2, hps, DK, DV), lambda b, h, dl: (b, 0, 0, h, 0, 0))
    r_shape = jax.ShapeDtypeStruct((nb * seq_len, HEADS * DV), BF16)
    r_spec = pl.BlockSpec((seq_len, DV * hps), lambda b, h, dl: (b, h))
    if latent:
        in_specs += [pl.BlockSpec((seq_len, DK), lambda b, h, dl: (0, 0)),
                     pl.BlockSpec((seq_len, DK), lambda b, h, dl: (0, 0)),
                     state_spec]
        args += [cos_t, sin_t, state0]
        out_specs, out_shape = r_spec, r_shape
    else:
        out_specs = (r_spec, state_spec)
        out_shape = (r_shape, jax.ShapeDtypeStruct((BATCH, 1, 2, HEADS, DK, DV), F32))
    return pl.pallas_call(
        functools.partial(_retention_kernel, seq_len=seq_len, latent=latent, hps=hps),
        grid_spec=pltpu.PrefetchScalarGridSpec(
            num_scalar_prefetch=1, grid=(nb, HEADS // hps), in_specs=in_specs,
            out_specs=out_specs, scratch_shapes=[pltpu.VMEM((seq_len, DV * hps), F32)]),
        out_shape=out_shape,
        compiler_params=_cparams(("arbitrary", "arbitrary"), 40),
        name="retention_latent" if latent else "retention_context",
    )(decay_logit, *args)


MIX_TM = 256


def _mix_kernel(rc_ref, rl_ref, wbr_ref, a_ref, mg0_ref, mg1_ref, wout_ref, xc_ref, xl_ref, g1_ref,
                sc2_ref, sh2_ref, npost_ref, npre_ref, rwt_ref, rb_ref,
                x1_ref, h2_ref, ti_ref, tw_ref):
    is_ctx = pl.program_id(0) * MIX_TM < N_CTX
    r = jnp.where(is_ctx, rc_ref[...], rl_ref[...])
    x = jnp.where(is_ctx, xc_ref[...], xl_ref[...])
    y = jnp.dot(r, wbr_ref[...], preferred_element_type=F32)
    mg = jnp.concatenate([mg0_ref[...], mg1_ref[...]], axis=1).astype(F32)
    mix = a_ref[...].astype(F32) + jax.nn.sigmoid(mg) * y
    out = jnp.dot(mix.astype(BF16), wout_ref[...], preferred_element_type=F32)
    x1 = x + g1_ref[...] * _rms(out, npost_ref[...])
    x1_ref[...] = x1
    h2 = _rms(x1, npre_ref[...]) * (1.0 + sc2_ref[...]) + sh2_ref[...]
    h2_ref[...] = h2
    logits = lax.dot_general(rwt_ref[...], h2.astype(BF16), (((1,), (1,)), ((), ())),
                             preferred_element_type=F32) + rb_ref[...]
    eidx = lax.broadcasted_iota(I32, logits.shape, 0).astype(F32)
    vals, idxs = [], []
    for _ in range(TOP_K):
        m = jnp.max(logits, axis=0, keepdims=True)
        am = jnp.min(jnp.where(logits == m, eidx, float(N_EXP)), axis=0, keepdims=True)
        vals.append(m)
        idxs.append(am)
        logits = jnp.where(eidx == am, -jnp.inf, logits)
    es = [jnp.exp(v - vals[0]) for v in vals]
    denom = es[0] + es[1] + es[2] + es[3]
    for kk in range(TOP_K):
        ti_ref[kk:kk + 1, :] = idxs[kk].astype(I32)
        tw_ref[kk:kk + 1, :] = es[kk] / denom


def _mix_route(r_ctx, r_lat, wbr_bf, a, proj, wout_bf, x_ctx, x_lat, mod3, npost, npre, rwt_bf, rb):
    tm = MIX_TM
    cw = 1024

    def full(shape):
        return pl.BlockSpec(shape, lambda i: (0,) * len(shape), pipeline_mode=pl.Buffered(1))

    def row(width):
        return pl.BlockSpec((tm, width), lambda i: (i, 0))

    return pl.pallas_call(
        _mix_kernel,
        grid=(N_TOK // tm,),
        in_specs=[*_ctx_lat_specs(tm, D),
                  full((D, D)), row(D),
                  pl.BlockSpec((tm, cw), lambda i: (i, COL_MGR // cw)),
                  pl.BlockSpec((tm, cw), lambda i: (i, COL_MGR // cw + 1)),
                  full((D, D)), *_ctx_lat_specs(tm, D),
                  _mod_spec(tm, MOD_G1), _mod_spec(tm, MOD_SC2), _mod_spec(tm, MOD_SH2),
                  full((1, D)), full((1, D)), full((N_EXP, D)), full((N_EXP, 1))],
        out_specs=(row(D), row(D),
                   pl.BlockSpec((TOP_K, tm), lambda i: (0, i)),
                   pl.BlockSpec((TOP_K, tm), lambda i: (0, i))),
        out_shape=(jax.ShapeDtypeStruct((N_TOK, D), F32),
                   jax.ShapeDtypeStruct((N_TOK, D), F32),
                   jax.ShapeDtypeStruct((TOP_K, N_TOK), I32),
                   jax.ShapeDtypeStruct((TOP_K, N_TOK), F32)),
        compiler_params=_cparams(("arbitrary",), 56),
        name="mix_out_route",
    )(r_ctx, r_lat, wbr_bf, a, proj, proj, wout_bf, x_ctx, x_lat, mod3, mod3, mod3,
      npost.reshape(1, D), npre.reshape(1, D), rwt_bf, rb.reshape(N_EXP, 1))


def _routing_tables(top_i):
    e = top_i.reshape(-1)
    onehot = e[:, None] == jnp.arange(N_EXP, dtype=I32)[None, :]
    blk = 256
    oh = onehot.astype(F32).reshape(N_PAIR // blk, blk, N_EXP)
    tri = (jnp.arange(blk)[:, None] >= jnp.arange(blk)[None, :]).astype(F32)
    within = jnp.einsum("ij,bjk->bik", tri, oh)
    blk_tot = within[:, -1, :]
    blk_off = jnp.cumsum(blk_tot, axis=0) - blk_tot
    csum = (within + blk_off[:, None, :]).reshape(N_PAIR, N_EXP)
    rank = jnp.sum(jnp.where(onehot, csum, 0.0), axis=1).astype(I32) - 1
    counts = csum[-1].astype(I32)
    tiles_e = (counts + MOE_TM - 1) // MOE_TM
    tile_end = jnp.cumsum(tiles_e)
    tile_start = tile_end - tiles_e
    slot = tile_start[e] * MOE_TM + rank
    n_tiles = tile_end[-1]
    tok = jnp.tile(jnp.arange(N_TOK, dtype=I32), TOP_K)
    tok_of_slot = jnp.zeros((MOE_ROWS,), I32).at[slot].set(tok, unique_indices=True)
    tid = jnp.arange(MOE_TILES, dtype=I32)
    tile_exp = jnp.minimum(jnp.sum((tid[:, None] >= tile_end[None, :]).astype(I32), axis=1),
                           N_EXP - 1)
    tile_rows = jnp.clip(counts[tile_exp] - MOE_TM * (tid - tile_start[tile_exp]), 0, MOE_TM)
    tile_rows = jnp.where(tid < n_tiles, tile_rows, 0).astype(I32)

    groups_e = (tiles_e + MOE_SUB - 1) // MOE_SUB
    g_end = jnp.cumsum(groups_e)
    g_start = g_end - groups_e
    n_groups = g_end[-1]
    gid = jnp.arange(MOE_GROUPS, dtype=I32)
    used = gid < n_groups
    g_exp = jnp.minimum(jnp.sum((gid[:, None] >= g_end[None, :]).astype(I32), axis=1), N_EXP - 1)
    g_exp = jnp.where(used, g_exp, g_exp[jnp.maximum(n_groups - 1, 0)])
    g_in_e = gid - g_start[g_exp]
    first = tile_start[g_exp] + MOE_SUB * g_in_e
    n_valid = jnp.where(used, jnp.clip(tiles_e[g_exp] - MOE_SUB * g_in_e, 0, MOE_SUB), 0)
    sub = jnp.arange(MOE_SUB, dtype=I32)[None, :]
    x_tile = first[:, None] + jnp.minimum(sub, jnp.maximum(n_valid[:, None] - 1, 0))
    x_tile = jnp.where(used[:, None], x_tile, x_tile[jnp.maximum(n_groups - 1, 0)][None, :])
    tail = n_tiles + (gid[:, None] - n_groups) * MOE_SUB + sub
    o_tile = jnp.where(used[:, None],
                       jnp.where(sub < n_valid[:, None], first[:, None] + sub, -1),
                       jnp.where(tail < MOE_TILES, tail, -1))
    return (slot, tok_of_slot, tile_rows, n_tiles.reshape(1).astype(I32),
            x_tile.reshape(-1).astype(I32),
            o_tile.reshape(-1).astype(I32), n_valid.astype(I32), g_exp.astype(I32))


def _row_copy(src_hbm, row, dst_ref, r, sem):
    return pltpu.make_async_copy(src_hbm.at[pl.ds(row, 1), :], dst_ref.at[pl.ds(r, 1), :], sem)


DISP_RING = 4
DISP_LAG = DISP_RING - 1
DISP_BLK = 8


def _dispatch_kernel(nt_ref, rows_ref, tos_ref, x_hbm, o_ref, buf, sem):
    j = pl.program_id(0)
    nt = nt_ref[0]
    done = j - DISP_LAG

    def row_blocks(t):
        return (rows_ref[t] + DISP_BLK - 1) // DISP_BLK

    @pl.when(j < nt)
    def _():
        b = j & (DISP_RING - 1)

        @pl.when(rows_ref[j] < MOE_TM)
        def _():
            buf[b] = jnp.zeros((MOE_TM, D), F32)

        def issue(blk, carry):
            for p in range(DISP_BLK):
                r = blk * DISP_BLK + p
                _row_copy(x_hbm, tos_ref[0, r], buf.at[b], r, sem.at[b]).start()
            return carry

        lax.fori_loop(0, row_blocks(j), issue, 0)

    @pl.when(jnp.logical_and(done >= 0, done < nt))
    def _():
        b = done & (DISP_RING - 1)

        def wait(blk, carry):
            for p in range(DISP_BLK):
                _row_copy(x_hbm, 0, buf.at[b], blk * DISP_BLK + p, sem.at[b]).wait()
            return carry

        lax.fori_loop(0, row_blocks(done), wait, 0)
        o_ref[...] = buf[b].astype(BF16)

    @pl.when(jnp.logical_and(done >= 0, done >= nt))
    def _():
        o_ref[...] = jnp.zeros_like(o_ref)


def _dispatch(h2, tok_of_slot, n_tiles, tile_rows):
    return pl.pallas_call(
        _dispatch_kernel,
        grid_spec=pltpu.PrefetchScalarGridSpec(
            num_scalar_prefetch=2, grid=(XS_TILES + DISP_LAG,),
            in_specs=[pl.BlockSpec((None, 1, MOE_TM),
                                   lambda j, nt, rows: (jnp.minimum(j, MOE_TILES - 1), 0, 0),
                                   memory_space=pltpu.SMEM),
                      pl.BlockSpec(memory_space=pl.ANY)],
            out_specs=pl.BlockSpec((MOE_TM, D),
                                   lambda j, nt, rows: (jnp.maximum(j - DISP_LAG, 0), 0)),
            scratch_shapes=[pltpu.VMEM((DISP_RING, MOE_TM, D), F32),
                            pltpu.SemaphoreType.DMA((DISP_RING,))]),
        out_shape=jax.ShapeDtypeStruct((XS_TILES * MOE_TM, D), BF16),
        compiler_params=_cparams(("arbitrary",), 32),
        name="moe_dispatch",
    )(n_tiles, tile_rows, tok_of_slot.reshape(MOE_TILES, 1, MOE_TM), h2)


def _tile_store(acc_ref, b, i, y_hbm, tile, sem):
    return pltpu.make_async_copy(acc_ref.at[b, i], y_hbm.at[pl.ds(tile * MOE_TM, MOE_TM), :],
                                 sem.at[b])


def _moe_kernel(xt_ref, ot_ref, nv_ref, ge_ref, *refs):
    (x_ref, wg_ref, wu_ref, wd_ref, bg_ref, bu_ref, bd_ref, y_hbm,
     wgb_ref, wub_ref, wdb_ref, acc_ref, sem) = refs
    g = pl.program_id(0)
    c = pl.program_id(1)
    nv = nv_ref[g]
    b = g & 1
    last_chunk = c == MOE_NCH - 1

    def wait_group(gw):
        for i in range(MOE_SUB):
            @pl.when(ot_ref[gw * MOE_SUB + i] >= 0)
            def _(i=i):
                _tile_store(acc_ref, gw & 1, i, y_hbm, 0, sem).wait()

    @pl.when(jnp.logical_and(c == 0, g >= 2))
    def _():
        wait_group(g - 2)

    @pl.when(jnp.logical_and(c == 0, nv > 0))
    def _():
        for i in range(MOE_SUB):
            acc_ref[b, i] = jnp.broadcast_to(bd_ref[...], (MOE_TM, D))

    def group_step(m):
        wgb_ref[...] = wg_ref[...].astype(BF16)
        wub_ref[...] = wu_ref[...].astype(BF16)
        wdb_ref[...] = wd_ref[...].astype(BF16)
        x = x_ref[:m * MOE_TM]
        gate = jnp.dot(x, wgb_ref[...], preferred_element_type=F32) + bg_ref[...]
        up = jnp.dot(x, wub_ref[...], preferred_element_type=F32) + bu_ref[...]
        gate = jnp.minimum(gate, LIMIT)
        up = jnp.clip(up, -LIMIT, LIMIT)
        act = (gate * jax.nn.sigmoid(ALPHA * gate) * (up + 1.0)).astype(BF16)
        for i in range(m):
            acc_ref[b, i] += jnp.dot(act[i * MOE_TM:(i + 1) * MOE_TM], wdb_ref[...],
                                     preferred_element_type=F32)

    for m in range(1, MOE_SUB + 1):
        pl.when(nv == m)(functools.partial(group_step, m))

    for i in range(MOE_SUB):
        @pl.when(jnp.logical_and(jnp.logical_and(i >= nv, last_chunk), ot_ref[g * MOE_SUB + i] >= 0))
        def _(i=i):
            acc_ref[b, i] = jnp.zeros((MOE_TM, D), F32)

        @pl.when(jnp.logical_and(last_chunk, ot_ref[g * MOE_SUB + i] >= 0))
        def _(i=i):
            _tile_store(acc_ref, b, i, y_hbm, ot_ref[g * MOE_SUB + i], sem).start()

    @pl.when(jnp.logical_and(last_chunk, g == MOE_GROUPS - 1))
    def _():
        wait_group(g - 1)
        wait_group(g)


def _moe(xs, wg, wu, wd, bg, bu, bd, x_tile, o_tile, n_valid, g_exp):
    def chunk_of(g, c, nv):
        return jnp.where(nv[g] > 0, c, MOE_NCH - 1)

    x_spec = pl.BlockSpec((pl.Element(MOE_SUB * MOE_TM), pl.Element(D)),
                          lambda g, c, xt, ot, nv, ge: (xt[g * MOE_SUB] * MOE_TM, 0))

    w_in_spec = pl.BlockSpec((None, D, MOE_CH),
                             lambda g, c, xt, ot, nv, ge: (ge[g], 0, chunk_of(g, c, nv)))
    b_in_spec = pl.BlockSpec((None, 1, MOE_CH),
                             lambda g, c, xt, ot, nv, ge: (ge[g], 0, chunk_of(g, c, nv)))
    w_out_spec = pl.BlockSpec((None, MOE_CH, D),
                              lambda g, c, xt, ot, nv, ge: (ge[g], chunk_of(g, c, nv), 0))
    b_out_spec = pl.BlockSpec((None, 1, D), lambda g, c, xt, ot, nv, ge: (ge[g], 0, 0))
    return pl.pallas_call(
        _moe_kernel,
        grid_spec=pltpu.PrefetchScalarGridSpec(
            num_scalar_prefetch=4, grid=(MOE_GROUPS, MOE_NCH),
            in_specs=[x_spec, w_in_spec, w_in_spec, w_out_spec, b_in_spec, b_in_spec, b_out_spec],
            out_specs=pl.BlockSpec(memory_space=pl.ANY),
            scratch_shapes=[pltpu.VMEM((D, MOE_CH), BF16), pltpu.VMEM((D, MOE_CH), BF16),
                            pltpu.VMEM((MOE_CH, D), BF16),
                            pltpu.VMEM((2, MOE_SUB, MOE_TM, D), F32),
                            pltpu.SemaphoreType.DMA((2,))]),
        out_shape=jax.ShapeDtypeStruct((MOE_ROWS, D), F32),
        compiler_params=_cparams(("arbitrary", "arbitrary"), 62),
        name="moe_experts",
    )(x_tile, o_tile, n_valid, g_exp, xs, wg, wu, wd,
      bg.reshape(N_EXP, 1, D_FF), bu.reshape(N_EXP, 1, D_FF), bd.reshape(N_EXP, 1, D))


COMB_TM = 256


COMB_TILES = N_TOK // COMB_TM
COMB_BLK = 8


def _combine_kernel(slot_ref, ys_hbm, w_ref, x1_ref, g2_ref, npost_ref, oc_ref, ol_ref, buf, sem):
    i = pl.program_id(0)

    def start_tile(b):
        def issue(blk, carry):
            for kk in range(TOP_K):
                for p in range(COMB_BLK):
                    r = blk * COMB_BLK + p
                    _row_copy(ys_hbm, slot_ref[0, kk * COMB_TM + r], buf.at[b, kk], r,
                              sem.at[b]).start()
            return carry

        lax.fori_loop(0, COMB_TM // COMB_BLK, issue, 0)

    def wait_tile(b):
        def wait(blk, carry):
            for kk in range(TOP_K):
                for p in range(COMB_BLK):
                    _row_copy(ys_hbm, 0, buf.at[b, kk], blk * COMB_BLK + p, sem.at[b]).wait()
            return carry

        lax.fori_loop(0, COMB_TM // COMB_BLK, wait, 0)

    for par in range(2):
        @pl.when(jnp.logical_and(i < COMB_TILES, (i & 1) == par))
        def _(par=par):
            start_tile(par)

    for par in range(2):
        @pl.when(jnp.logical_and(i >= 1, ((i - 1) & 1) == par))
        def _(par=par):
            wait_tile(par)

    @pl.when(i >= 1)
    def _():
        b = (i - 1) & 1
        w = w_ref[...]
        f = buf[b, 0] * w[:, 0:1]
        for kk in range(1, TOP_K):
            f = f + buf[b, kk] * w[:, kk:kk + 1]
        out = x1_ref[...] + g2_ref[...] * _rms(f, npost_ref[...])
        base = (i - 1) * COMB_TM

        @pl.when(base < N_CTX)
        def _():
            oc_ref[...] = out

        @pl.when(base >= N_CTX)
        def _():
            ol_ref[...] = out


def _combine(ys, slot, top_w_t, x1, mod3, npost):
    tm = COMB_TM

    def prev(i):
        return jnp.maximum(i - 1, 0)

    oc_spec, ol_spec = _ctx_lat_specs(tm, D)
    return pl.pallas_call(
        _combine_kernel,
        grid=(COMB_TILES + 1,),
        in_specs=[pl.BlockSpec((None, 1, TOP_K * tm),
                               lambda i: (jnp.minimum(i, COMB_TILES - 1), 0, 0),
                               memory_space=pltpu.SMEM),
                  pl.BlockSpec(memory_space=pl.ANY),
                  pl.BlockSpec((tm, TOP_K), lambda i: (prev(i), 0)),
                  pl.BlockSpec((tm, D), lambda i: (prev(i), 0)),
                  pl.BlockSpec((None, 1, D), lambda i: (_mod_row(prev(i) * tm), 0, MOD_G2)),
                  pl.BlockSpec((1, D), lambda i: (0, 0))],
        out_specs=(pl.BlockSpec((tm, D), lambda i: oc_spec.index_map(prev(i))),
                   pl.BlockSpec((tm, D), lambda i: ol_spec.index_map(prev(i)))),
        scratch_shapes=[pltpu.VMEM((2, TOP_K, tm, D), F32), pltpu.SemaphoreType.DMA((2,))],
        out_shape=(jax.ShapeDtypeStruct((N_CTX, D), F32), jax.ShapeDtypeStruct((N_LAT, D), F32)),
        compiler_params=_cparams(("arbitrary",), 48),
        name="moe_combine",
    )(slot.reshape(TOP_K, COMB_TILES, tm).transpose(1, 0, 2).reshape(COMB_TILES, 1, TOP_K * tm),
      ys, top_w_t, x1, mod3, npost.reshape(1, D))


def _rope_tables():
    pos = jnp.arange(DEC_SEQ)
    row = (pos // GRID_W).astype(F32)
    col = (pos % GRID_W).astype(F32)
    n_freq = DK // 4
    freqs = THETA ** (-jnp.arange(n_freq, dtype=F32) / n_freq)
    ar = row[:, None] * freqs
    ac = col[:, None] * freqs
    cos_t = jnp.concatenate([jnp.cos(ar), jnp.cos(ar), jnp.cos(ac), jnp.cos(ac)], axis=1)
    sin_t = jnp.concatenate([-jnp.sin(ar), jnp.sin(ar), -jnp.sin(ac), jnp.sin(ac)], axis=1)
    return cos_t.astype(F32), sin_t.astype(F32)


def kernel(x_prompt, x_sample, state_ret, c, c_ctx, w_ada, b_ada, norm_mix_pre, norm_mix_post,
           norm_ffn_pre, norm_ffn_post, w_in, conv_w, conv_b, ret_decay_logit, ret_gn_w,
           w_br_conv, w_br_ret, w_out, router_w, router_b, moe_w_gate, moe_b_gate,
           moe_w_up, moe_b_up, moe_w_down, moe_b_down):
    layer = 0
    x_ctx = x_prompt.reshape(N_CTX, D)
    x_lat = x_sample.reshape(N_LAT, D)
    cvec = jnp.concatenate(
        [c_ctx[None, :], c, jnp.zeros((MOD_ROWS - 1 - DEC_BATCH, D), F32)], axis=0)

    mod = _modulation(cvec, w_ada[layer], b_ada[layer])
    mod3 = mod.reshape(MOD_ROWS, 1, N_MOD * D)

    h1 = _norm_mod(x_ctx, x_lat, norm_mix_pre[layer], mod3)
    proj = _in_proj(h1, w_in[layer])

    a = _conv_branch(proj, conv_w[layer], conv_b[layer], w_br_conv[layer].astype(BF16))

    cos_t, sin_t = _rope_tables()
    dl = ret_decay_logit[layer].astype(F32)
    r_ctx, new_state = _retention(proj, dl, ret_gn_w[layer], None, None, None, latent=False)
    r_lat = _retention(proj, dl, ret_gn_w[layer], cos_t, sin_t, state_ret, latent=True)

    x1, h2, top_i, top_w = _mix_route(
        r_ctx, r_lat, w_br_ret[layer].astype(BF16), a, proj, w_out[layer].astype(BF16),
        x_ctx, x_lat, mod3, norm_mix_post[layer], norm_ffn_pre[layer],
        router_w[layer].T.astype(BF16), router_b[layer])

    slot, tok_of_slot, tile_rows, n_tiles, x_tile, o_tile, n_valid, g_exp = _routing_tables(top_i)
    xs = _dispatch(h2, tok_of_slot, n_tiles, tile_rows)
    ys = _moe(xs, moe_w_gate[layer], moe_w_up[layer], moe_w_down[layer],
              moe_b_gate[layer], moe_b_up[layer], moe_b_down[layer],
              x_tile, o_tile, n_valid, g_exp)
    y_ctx, y_lat = _combine(ys, slot, top_w.T, x1, mod3, norm_ffn_post[layer])

    return (y_ctx.reshape(BATCH, SEQ, D), y_lat.reshape(DEC_BATCH, DEC_SEQ, D), new_state)
```

```python
import functools

import jax
import jax.numpy as jnp
import numpy as np
from jax import lax
from jax.experimental import pallas as pl
from jax.experimental.pallas import tpu as pltpu

F32 = jnp.float32
BF16 = jnp.bfloat16
I32 = jnp.int32

D = 2048
BATCH = 16
SEQ = 256
DEC_BATCH = 2
DEC_SEQ = 1024
GRID_W = 64
CONV_W = 1024
HEADS = 8
DK = 128
DV = 256
CHUNK = 128
N_EXP = 32
TOP_K = 4
D_FF = 2048
LIMIT = 7.0
ALPHA = 1.702
THETA = 10000.0
EPS = 1e-6
N_MOD = 6
N_CTX = BATCH * SEQ
N_LAT = DEC_BATCH * DEC_SEQ
N_TOK = N_CTX + N_LAT
IN_COLS = 3 * CONV_W + 2 * HEADS * DK + 2 * HEADS * DV + 2 * D
COL_CX, COL_CB, COL_CC = 0, CONV_W, 2 * CONV_W
COL_Q = 3 * CONV_W
COL_K = COL_Q + HEADS * DK
COL_V = COL_K + HEADS * DK
COL_G = COL_V + HEADS * DV
COL_MGC = COL_G + HEADS * DV
COL_MGR = COL_MGC + D
MOD_SH1, MOD_SC1, MOD_G1, MOD_SH2, MOD_SC2, MOD_G2 = range(6)
MOD_ROWS = 8

MOE_TM = 256
N_PAIR = N_TOK * TOP_K
MOE_TILES = (N_PAIR + N_EXP * (MOE_TM - 1)) // MOE_TM + 1
MOE_ROWS = MOE_TILES * MOE_TM
MOE_SUB = 4
MOE_GROUPS = (MOE_TILES + (MOE_SUB - 1) * N_EXP) // MOE_SUB
MOE_CH = 512
MOE_NCH = D_FF // MOE_CH
XS_TILES = MOE_TILES + MOE_SUB - 1

MIB = 1024 * 1024


def _cparams(sem, vmem_mib):
    return pltpu.CompilerParams(dimension_semantics=sem, vmem_limit_bytes=vmem_mib * MIB)


def _mod_row(t0):
    return jnp.where(t0 < N_CTX, 0, 1 + (t0 - N_CTX) // DEC_SEQ)


def _mod_spec(tm, chunk):
    return pl.BlockSpec((None, 1, D), lambda i: (_mod_row(i * tm), 0, chunk))


def _rms(x, g):
    return (x * lax.rsqrt(jnp.mean(x * x, axis=-1, keepdims=True) + EPS)) * g


def _ada_kernel(c_ref, w_ref, b_ref, o_ref):
    c = c_ref[...]
    s = c * jax.nn.sigmoid(c)
    o_ref[...] = jnp.dot(s.astype(BF16), w_ref[...].astype(BF16),
                         preferred_element_type=F32) + b_ref[...]


def _modulation(cvec, w_ada, b_ada):
    tn = 1024
    return pl.pallas_call(
        _ada_kernel,
        grid=(N_MOD * D // tn,),
        in_specs=[pl.BlockSpec((MOD_ROWS, D), lambda j: (0, 0)),
                  pl.BlockSpec((D, tn), lambda j: (0, j)),
                  pl.BlockSpec((1, tn), lambda j: (0, j))],
        out_specs=pl.BlockSpec((MOD_ROWS, tn), lambda j: (0, j)),
        out_shape=jax.ShapeDtypeStruct((MOD_ROWS, N_MOD * D), F32),
        compiler_params=_cparams(("arbitrary",), 40),
        name="ada_modulation",
    )(cvec, w_ada, b_ada.reshape(1, -1))


NORM_TM = 512


def _ctx_lat_specs(tm, width):
    ctx_tiles = N_CTX // tm
    return (pl.BlockSpec((tm, width), lambda i, *_: (jnp.minimum(i, ctx_tiles - 1), 0)),
            pl.BlockSpec((tm, width), lambda i, *_: (jnp.maximum(i - ctx_tiles, 0), 0)))


def _norm_mod_kernel(xc_ref, xl_ref, g_ref, sh_ref, sc_ref, o_ref):
    x = jnp.where(pl.program_id(0) * NORM_TM < N_CTX, xc_ref[...], xl_ref[...])
    y = _rms(x, g_ref[...])
    o_ref[...] = (y * (1.0 + sc_ref[...]) + sh_ref[...]).astype(BF16)


def _norm_mod(x_ctx, x_lat, g, mod3):
    tm = NORM_TM
    return pl.pallas_call(
        _norm_mod_kernel,
        grid=(N_TOK // tm,),
        in_specs=[*_ctx_lat_specs(tm, D),
                  pl.BlockSpec((1, D), lambda i: (0, 0)),
                  _mod_spec(tm, MOD_SH1),
                  _mod_spec(tm, MOD_SC1)],
        out_specs=pl.BlockSpec((tm, D), lambda i: (i, 0)),
        out_shape=jax.ShapeDtypeStruct((N_TOK, D), BF16),
        compiler_params=_cparams(("arbitrary",), 40),
        name="norm_modulate",
    )(x_ctx, x_lat, g.reshape(1, D), mod3, mod3)


def _proj_kernel(h_ref, w_ref, o_ref, wb_ref):
    @pl.when(pl.program_id(1) == 0)
    def _():
        wb_ref[...] = w_ref[...].astype(BF16)

    o_ref[...] = jnp.dot(h_ref[...], wb_ref[...], preferred_element_type=F32).astype(BF16)


def _in_proj(h, w_in):
    tm, tn = 2048, 1024
    return pl.pallas_call(
        _proj_kernel,
        grid=(IN_COLS // tn, N_TOK // tm),
        in_specs=[pl.BlockSpec((tm, D), lambda n, m: (m, 0)),
                  pl.BlockSpec((D, tn), lambda n, m: (0, n))],
        out_specs=pl.BlockSpec((tm, tn), lambda n, m: (m, n)),
        out_shape=jax.ShapeDtypeStruct((N_TOK, IN_COLS), BF16),
        scratch_shapes=[pltpu.VMEM((D, tn), BF16)],
        compiler_params=_cparams(("arbitrary", "arbitrary"), 60),
        name="in_proj",
    )(h, w_in)


CONV_TM = 256


def _conv_kernel(cx_ref, cb_ref, cc_ref, mg0_ref, mg1_ref, cw_ref, cbias_ref, w_ref, o_ref):
    i = pl.program_id(0)
    u = cc_ref[...].astype(F32) * cx_ref[...].astype(F32)
    seg = jnp.where(i * CONV_TM < N_CTX, SEQ, GRID_W)
    pos = lax.broadcasted_iota(I32, (CONV_TM, 1), 0) & (seg - 1)
    prev = jnp.where(pos == 0, 0.0, pltpu.roll(u, 1, 0))
    nxt = jnp.where(pos == seg - 1, 0.0, pltpu.roll(u, CONV_TM - 1, 0))
    cw = cw_ref[...]
    uc = prev * cw[0:1, :] + u * cw[1:2, :] + nxt * cw[2:3, :] + cbias_ref[...]
    z = (cb_ref[...].astype(F32) * uc).astype(BF16)
    y = jnp.dot(z, w_ref[...], preferred_element_type=F32)
    mg = jnp.concatenate([mg0_ref[...], mg1_ref[...]], axis=1).astype(F32)
    o_ref[...] = (jax.nn.sigmoid(mg) * y).astype(BF16)


def _conv_branch(proj, conv_w, conv_b, w_br_conv_bf):
    tm = CONV_TM
    cw = 1024

    def col(c):
        return pl.BlockSpec((tm, cw), lambda i: (i, c))

    return pl.pallas_call(
        _conv_kernel,
        grid=(N_TOK // tm,),
        in_specs=[col(COL_CX // cw), col(COL_CB // cw), col(COL_CC // cw),
                  col(COL_MGC // cw), col(COL_MGC // cw + 1),
                  pl.BlockSpec((3, CONV_W), lambda i: (0, 0)),
                  pl.BlockSpec((1, CONV_W), lambda i: (0, 0)),
                  pl.BlockSpec((CONV_W, D), lambda i: (0, 0))],
        out_specs=pl.BlockSpec((tm, D), lambda i: (i, 0)),
        out_shape=jax.ShapeDtypeStruct((N_TOK, D), BF16),
        compiler_params=_cparams(("arbitrary",), 48),
        name="conv_branch",
    )(proj, proj, proj, proj, proj, conv_w, conv_b.reshape(1, CONV_W), w_br_conv_bf)


def _log_sigmoid(x):
    return -(jnp.maximum(-x, 0.0) + jnp.log1p(jnp.exp(-jnp.abs(x))))


def _rope(x, cos, sin_signed):
    lane = lax.broadcasted_iota(I32, x.shape, 1)
    swapped = jnp.where((lane & 63) < 32, pltpu.roll(x, 96, 1), pltpu.roll(x, 32, 1))
    return x * cos + swapped * sin_signed


def _retention_kernel(*refs, seq_len, latent, hps):
    for hh in range(hps):
        _retention_head(refs, pl.program_id(1) * hps + hh, hh, seq_len=seq_len, latent=latent)


def _retention_head(refs, h, hh, *, seq_len, latent):
    if latent:
        (dl_ref, q_ref, k_ref, v_ref, g_ref, gn_ref, cos_ref, sin_ref, s0_ref,
         r_ref, o_acc) = refs
    else:
        dl_ref, q_ref, k_ref, v_ref, g_ref, gn_ref, r_ref, sout_ref, o_acc = refs
    c = CHUNK
    n_chunks = seq_len // c
    kcol = slice(hh * DK, (hh + 1) * DK)
    vcol = slice(hh * DV, (hh + 1) * DV)

    lgf = _log_sigmoid(jnp.full((c, c), dl_ref[0, h], F32))
    lgb = _log_sigmoid(jnp.full((c, c), dl_ref[1, h], F32))
    lgf_w = _log_sigmoid(jnp.full((c, DV), dl_ref[0, h], F32))
    lgb_w = _log_sigmoid(jnp.full((c, DV), dl_ref[1, h], F32))

    ii = lax.broadcasted_iota(I32, (c, c), 0).astype(F32)
    jj = lax.broadcasted_iota(I32, (c, c), 1).astype(F32)
    diff = ii - jj
    m_in = jnp.where(diff >= 0.0, jnp.exp(jnp.maximum(diff, 0.0) * lgf),
                     jnp.exp(jnp.maximum(-diff, 0.0) * lgb))
    io = lax.broadcasted_iota(I32, (c, DV), 0).astype(F32)
    dq_f = jnp.exp((io + 1.0) * lgf_w)
    dq_b = jnp.exp((c - io) * lgb_w)
    dk_f = jnp.exp((c - 1.0 - jj) * lgf)
    dk_b = jnp.exp(jj * lgb)
    dc_f = jnp.exp(c * lgf_w)
    dc_b = jnp.exp(c * lgb_w)

    q = q_ref[:, kcol].astype(F32)
    k = k_ref[:, kcol].astype(F32) * (DK ** -0.5)
    if latent:
        q = _rope(q, cos_ref[...], sin_ref[...])
        k = _rope(k, cos_ref[...], sin_ref[...])
        s_f = s0_ref[0, hh]
        s_b = s0_ref[1, hh]
    else:
        s_f = jnp.zeros((DK, DV), F32)
        s_b = jnp.zeros((DK, DV), F32)
    qb = q.astype(BF16)

    def chunk(x, n):
        return x[n * c:(n + 1) * c]

    for n in range(n_chunks):
        qn = chunk(qb, n)
        kt = chunk(k, n).T
        vn = v_ref[n * c:(n + 1) * c, vcol]
        s = jnp.dot(qn, kt.astype(BF16), preferred_element_type=F32)
        p = (s * m_in).astype(BF16)
        o = jnp.dot(p, vn, preferred_element_type=F32)
        o = o + jnp.dot(qn, s_f.astype(BF16), preferred_element_type=F32) * dq_f
        o_acc[n * c:(n + 1) * c, vcol] = o
        s_f = dc_f * s_f + jnp.dot((kt * dk_f).astype(BF16), vn, preferred_element_type=F32)

    for n in reversed(range(n_chunks)):
        qn = chunk(qb, n)
        kt = chunk(k, n).T
        vn = v_ref[n * c:(n + 1) * c, vcol]
        o_acc[n * c:(n + 1) * c, vcol] += (
            jnp.dot(qn, s_b.astype(BF16), preferred_element_type=F32) * dq_b)
        s_b = dc_b * s_b + jnp.dot((kt * dk_b).astype(BF16), vn, preferred_element_type=F32)

    if not latent:
        sout_ref[0, hh] = s_f
        sout_ref[1, hh] = s_b

    o = o_acc[:, vcol]
    mu = jnp.mean(o, axis=-1, keepdims=True)
    oc = o - mu
    var = jnp.mean(oc * oc, axis=-1, keepdims=True)
    on = oc * lax.rsqrt(var + EPS) * gn_ref[:, vcol]
    g = g_ref[:, vcol].astype(F32)
    r_ref[:, vcol] = ((g * jax.nn.sigmoid(g)) * on).astype(BF16)


def _retention(proj, decay_logit, gn_w, cos_t, sin_t, state0, *, latent):
    seq_len = DEC_SEQ if latent else SEQ
    nb = DEC_BATCH if latent else BATCH
    rb0 = (N_CTX // DEC_SEQ) if latent else 0

    hps = 1 if latent else 4
    assert COL_V % (DV * hps) == 0 and COL_G % (DV * hps) == 0 and COL_Q % (DK * hps) == 0

    def spec(width, col0):
        w = width * hps
        return pl.BlockSpec((seq_len, w), lambda b, h, dl: (rb0 + b, col0 // w + h))

    in_specs = [spec(DK, COL_Q), spec(DK, COL_K), spec(DV, COL_V), spec(DV, COL_G),
                pl.BlockSpec((1, DV * hps), lambda b, h, dl: (0, h))]
    args = [proj, proj, proj, proj, gn_w.reshape(1, HEADS * DV)]
    state_spec = pl.BlockSpec((None, None, 2, hps, DK, DV), lambda b, h, dl: (b, 0, 0, h, 0, 0))
    r_shape = jax.ShapeDtypeStruct((nb * seq_len, HEADS * DV), BF16)
    r_spec = pl.BlockSpec((seq_len, DV * hps), lambda b, h, dl: (b, h))
    if latent:
        in_specs += [pl.BlockSpec((seq_len, DK), lambda b, h, dl: (0, 0)),
                     pl.BlockSpec((seq_len, DK), lambda b, h, dl: (0, 0)),
                     state_spec]
        args += [cos_t, sin_t, state0]
        out_specs, out_shape = r_spec, r_shape
    else:
        out_specs = (r_spec, state_spec)
        out_shape = (r_shape, jax.ShapeDtypeStruct((BATCH, 1, 2, HEADS, DK, DV), F32))
    return pl.pallas_call(
        functools.partial(_retention_kernel, seq_len=seq_len, latent=latent, hps=hps),
        grid_spec=pltpu.PrefetchScalarGridSpec(
            num_scalar_prefetch=1, grid=(nb, HEADS // hps), in_specs=in_specs,
            out_specs=out_specs, scratch_shapes=[pltpu.VMEM((seq_len, DV * hps), F32)]),
        out_shape=out_shape,
        compiler_params=_cparams(("arbitrary", "arbitrary"), 40),
        name="retention_latent" if latent else "retention_context",
    )(decay_logit, *args)


MIX_TM = 256


def _mix_kernel(rc_ref, rl_ref, wbr_ref, a_ref, mg0_ref, mg1_ref, wout_ref, xc_ref, xl_ref, g1_ref,
                sc2_ref, sh2_ref, npost_ref, npre_ref, rwt_ref, rb_ref,
                x1_ref, h2_ref, ti_ref, tw_ref):
    is_ctx = pl.program_id(0) * MIX_TM < N_CTX
    r = jnp.where(is_ctx, rc_ref[...], rl_ref[...])
    x = jnp.where(is_ctx, xc_ref[...], xl_ref[...])
    y = jnp.dot(r, wbr_ref[...], preferred_element_type=F32)
    mg = jnp.concatenate([mg0_ref[...], mg1_ref[...]], axis=1).astype(F32)
    mix = a_ref[...].astype(F32) + jax.nn.sigmoid(mg) * y
    out = jnp.dot(mix.astype(BF16), wout_ref[...], preferred_element_type=F32)
    x1 = x + g1_ref[...] * _rms(out, npost_ref[...])
    x1_ref[...] = x1
    h2 = _rms(x1, npre_ref[...]) * (1.0 + sc2_ref[...]) + sh2_ref[...]
    h2_ref[...] = h2
    logits = lax.dot_general(rwt_ref[...], h2.astype(BF16), (((1,), (1,)), ((), ())),
                             preferred_element_type=F32) + rb_ref[...]
    eidx = lax.broadcasted_iota(I32, logits.shape, 0).astype(F32)
    vals, idxs = [], []
    for _ in range(TOP_K):
        m = jnp.max(logits, axis=0, keepdims=True)
        am = jnp.min(jnp.where(logits == m, eidx, float(N_EXP)), axis=0, keepdims=True)
        vals.append(m)
        idxs.append(am)
        logits = jnp.where(eidx == am, -jnp.inf, logits)
    es = [jnp.exp(v - vals[0]) for v in vals]
    denom = es[0] + es[1] + es[2] + es[3]
    for kk in range(TOP_K):
        ti_ref[kk:kk + 1, :] = idxs[kk].astype(I32)
        tw_ref[kk:kk + 1, :] = es[kk] / denom


def _mix_route(r_ctx, r_lat, wbr_bf, a, proj, wout_bf, x_ctx, x_lat, mod3, npost, npre, rwt_bf, rb):
    tm = MIX_TM
    cw = 1024

    def full(shape):
        return pl.BlockSpec(shape, lambda i: (0,) * len(shape), pipeline_mode=pl.Buffered(1))

    def row(width):
        return pl.BlockSpec((tm, width), lambda i: (i, 0))

    return pl.pallas_call(
        _mix_kernel,
        grid=(N_TOK // tm,),
        in_specs=[*_ctx_lat_specs(tm, D),
                  full((D, D)), row(D),
                  pl.BlockSpec((tm, cw), lambda i: (i, COL_MGR // cw)),
                  pl.BlockSpec((tm, cw), lambda i: (i, COL_MGR // cw + 1)),
                  full((D, D)), *_ctx_lat_specs(tm, D),
                  _mod_spec(tm, MOD_G1), _mod_spec(tm, MOD_SC2), _mod_spec(tm, MOD_SH2),
                  full((1, D)), full((1, D)), full((N_EXP, D)), full((N_EXP, 1))],
        out_specs=(row(D), row(D),
                   pl.BlockSpec((TOP_K, tm), lambda i: (0, i)),
                   pl.BlockSpec((TOP_K, tm), lambda i: (0, i))),
        out_shape=(jax.ShapeDtypeStruct((N_TOK, D), F32),
                   jax.ShapeDtypeStruct((N_TOK, D), F32),
                   jax.ShapeDtypeStruct((TOP_K, N_TOK), I32),
                   jax.ShapeDtypeStruct((TOP_K, N_TOK), F32)),
        compiler_params=_cparams(("arbitrary",), 56),
        name="mix_out_route",
    )(r_ctx, r_lat, wbr_bf, a, proj, proj, wout_bf, x_ctx, x_lat, mod3, mod3, mod3,
      npost.reshape(1, D), npre.reshape(1, D), rwt_bf, rb.reshape(N_EXP, 1))


def _routing_tables(top_i):
    e = top_i.reshape(-1)
    onehot = e[:, None] == jnp.arange(N_EXP, dtype=I32)[None, :]
    blk = 256
    oh = onehot.astype(F32).reshape(N_PAIR // blk, blk, N_EXP)
    tri = (jnp.arange(blk)[:, None] >= jnp.arange(blk)[None, :]).astype(F32)
    within = jnp.einsum("ij,bjk->bik", tri, oh)
    blk_tot = within[:, -1, :]
    blk_off = jnp.cumsum(blk_tot, axis=0) - blk_tot
    csum = (within + blk_off[:, None, :]).reshape(N_PAIR, N_EXP)
    rank = jnp.sum(jnp.where(onehot, csum, 0.0), axis=1).astype(I32) - 1
    counts = csum[-1].astype(I32)
    tiles_e = (counts + MOE_TM - 1) // MOE_TM
    tile_end = jnp.cumsum(tiles_e)
    tile_start = tile_end - tiles_e
    slot = tile_start[e] * MOE_TM + rank
    n_tiles = tile_end[-1]
    tok = jnp.tile(jnp.arange(N_TOK, dtype=I32), TOP_K)
    tok_of_slot = jnp.zeros((MOE_ROWS,), I32).at[slot].set(tok, unique_indices=True)
    tid = jnp.arange(MOE_TILES, dtype=I32)
    tile_exp = jnp.minimum(jnp.sum((tid[:, None] >= tile_end[None, :]).astype(I32), axis=1),
                           N_EXP - 1)
    tile_rows = jnp.clip(counts[tile_exp] - MOE_TM * (tid - tile_start[tile_exp]), 0, MOE_TM)
    tile_rows = jnp.where(tid < n_tiles, tile_rows, 0).astype(I32)

    groups_e = (tiles_e + MOE_SUB - 1) // MOE_SUB
    g_end = jnp.cumsum(groups_e)
    g_start = g_end - groups_e
    n_groups = g_end[-1]
    gid = jnp.arange(MOE_GROUPS, dtype=I32)
    used = gid < n_groups
    g_exp = jnp.minimum(jnp.sum((gid[:, None] >= g_end[None, :]).astype(I32), axis=1), N_EXP - 1)
    g_exp = jnp.where(used, g_exp, g_exp[jnp.maximum(n_groups - 1, 0)])
    g_in_e = gid - g_start[g_exp]
    first = tile_start[g_exp] + MOE_SUB * g_in_e
    n_valid = jnp.where(used, jnp.clip(tiles_e[g_exp] - MOE_SUB * g_in_e, 0, MOE_SUB), 0)
    sub = jnp.arange(MOE_SUB, dtype=I32)[None, :]
    x_tile = first[:, None] + jnp.minimum(sub, jnp.maximum(n_valid[:, None] - 1, 0))
    x_tile = jnp.where(used[:, None], x_tile, x_tile[jnp.maximum(n_groups - 1, 0)][None, :])
    tail = n_tiles + (gid[:, None] - n_groups) * MOE_SUB + sub
    o_tile = jnp.where(used[:, None],
                       jnp.where(sub < n_valid[:, None], first[:, None] + sub, -1),
                       jnp.where(tail < MOE_TILES, tail, -1))
    return (slot, tok_of_slot, tile_rows, n_tiles.reshape(1).astype(I32),
            x_tile.reshape(-1).astype(I32),
            o_tile.reshape(-1).astype(I32), n_valid.astype(I32), g_exp.astype(I32))


def _row_copy(src_hbm, row, dst_ref, r, sem):
    return pltpu.make_async_copy(src_hbm.at[pl.ds(row, 1), :], dst_ref.at[pl.ds(r, 1), :], sem)


DISP_RING = 4
DISP_LAG = DISP_RING - 1
DISP_BLK = 8


def _dispatch_kernel(nt_ref, rows_ref, tos_ref, x_hbm, o_ref, buf, sem):
    j = pl.program_id(0)
    nt = nt_ref[0]
    done = j - DISP_LAG

    def row_blocks(t):
        return (rows_ref[t] + DISP_BLK - 1) // DISP_BLK

    def start_tile(b):
        @pl.when(rows_ref[j] < MOE_TM)
        def _():
            buf[b] = jnp.zeros((MOE_TM, D), F32)

        def issue(blk, carry):
            for p in range(DISP_BLK):
                r = blk * DISP_BLK + p
                _row_copy(x_hbm, tos_ref[0, r], buf.at[b], r, sem.at[b]).start()
            return carry

        lax.fori_loop(0, row_blocks(j), issue, 0)

    def finish_tile(b):
        def wait(blk, carry):
            for p in range(DISP_BLK):
                _row_copy(x_hbm, 0, buf.at[b], blk * DISP_BLK + p, sem.at[b]).wait()
            return carry

        lax.fori_loop(0, row_blocks(done), wait, 0)
        o_ref[...] = buf[b].astype(BF16)

    for slot in range(DISP_RING):
        @pl.when(jnp.logical_and(j < nt, (j & (DISP_RING - 1)) == slot))
        def _(slot=slot):
            start_tile(slot)

    for slot in range(DISP_RING):
        @pl.when(jnp.logical_and(jnp.logical_and(done >= 0, done < nt),
                                 (done & (DISP_RING - 1)) == slot))
        def _(slot=slot):
            finish_tile(slot)

    @pl.when(jnp.logical_and(done >= 0, done >= nt))
    def _():
        o_ref[...] = jnp.zeros_like(o_ref)


def _dispatch(h2, tok_of_slot, n_tiles, tile_rows):
    return pl.pallas_call(
        _dispatch_kernel,
        grid_spec=pltpu.PrefetchScalarGridSpec(
            num_scalar_prefetch=2, grid=(XS_TILES + DISP_LAG,),
            in_specs=[pl.BlockSpec((None, 1, MOE_TM),
                                   lambda j, nt, rows: (jnp.minimum(j, MOE_TILES - 1), 0, 0),
                                   memory_space=pltpu.SMEM),
                      pl.BlockSpec(memory_space=pl.ANY)],
            out_specs=pl.BlockSpec((MOE_TM, D),
                                   lambda j, nt, rows: (jnp.maximum(j - DISP_LAG, 0), 0)),
            scratch_shapes=[pltpu.VMEM((DISP_RING, MOE_TM, D), F32),
                            pltpu.SemaphoreType.DMA((DISP_RING,))]),
        out_shape=jax.ShapeDtypeStruct((XS_TILES * MOE_TM, D), BF16),
        compiler_params=_cparams(("arbitrary",), 32),
        name="moe_dispatch",
    )(n_tiles, tile_rows, tok_of_slot.reshape(MOE_TILES, 1, MOE_TM), h2)


def _tile_store(acc_ref, b, i, y_hbm, tile, sem):
    return pltpu.make_async_copy(acc_ref.at[b, i], y_hbm.at[pl.ds(tile * MOE_TM, MOE_TM), :],
                                 sem.at[b])


def _moe_kernel(xt_ref, ot_ref, nv_ref, ge_ref, *refs):
    (x_ref, wg_ref, wu_ref, wd_ref, bg_ref, bu_ref, bd_ref, y_hbm,
     wgb_ref, wub_ref, wdb_ref, acc_ref, sem) = refs
    g = pl.program_id(0)
    c = pl.program_id(1)
    nv = nv_ref[g]
    b = g & 1
    last_chunk = c == MOE_NCH - 1

    def wait_group(gw):
        for i in range(MOE_SUB):
            @pl.when(ot_ref[gw * MOE_SUB + i] >= 0)
            def _(i=i):
                _tile_store(acc_ref, gw & 1, i, y_hbm, 0, sem).wait()

    @pl.when(jnp.logical_and(c == 0, g >= 2))
    def _():
        wait_group(g - 2)

    @pl.when(jnp.logical_and(c == 0, nv > 0))
    def _():
        for i in range(MOE_SUB):
            acc_ref[b, i] = jnp.broadcast_to(bd_ref[...], (MOE_TM, D))

    def group_step(m):
        wgb_ref[...] = wg_ref[...].astype(BF16)
        wub_ref[...] = wu_ref[...].astype(BF16)
        wdb_ref[...] = wd_ref[...].astype(BF16)
        x = x_ref[:m * MOE_TM]
        gate = jnp.dot(x, wgb_ref[...], preferred_element_type=F32) + bg_ref[...]
        up = jnp.dot(x, wub_ref[...], preferred_element_type=F32) + bu_ref[...]
        gate = jnp.minimum(gate, LIMIT)
        up = jnp.clip(up, -LIMIT, LIMIT)
        act = (gate * jax.nn.sigmoid(ALPHA * gate) * (up + 1.0)).astype(BF16)
        for i in range(m):
            acc_ref[b, i] += jnp.dot(act[i * MOE_TM:(i + 1) * MOE_TM], wdb_ref[...],
                                     preferred_element_type=F32)

    for m in range(1, MOE_SUB + 1):
        pl.when(nv == m)(functools.partial(group_step, m))

    for i in range(MOE_SUB):
        @pl.when(jnp.logical_and(jnp.logical_and(i >= nv, last_chunk), ot_ref[g * MOE_SUB + i] >= 0))
        def _(i=i):
            acc_ref[b, i] = jnp.zeros((MOE_TM, D), F32)

        @pl.when(jnp.logical_and(last_chunk, ot_ref[g * MOE_SUB + i] >= 0))
        def _(i=i):
            _tile_store(acc_ref, b, i, y_hbm, ot_ref[g * MOE_SUB + i], sem).start()

    @pl.when(jnp.logical_and(last_chunk, g == MOE_GROUPS - 1))
    def _():
        wait_group(g - 1)
        wait_group(g)


def _moe(xs, wg, wu, wd, bg, bu, bd, x_tile, o_tile, n_valid, g_exp):
    def chunk_of(g, c, nv):
        return jnp.where(nv[g] > 0, c, MOE_NCH - 1)

    x_spec = pl.BlockSpec((pl.Element(MOE_SUB * MOE_TM), pl.Element(D)),
                          lambda g, c, xt, ot, nv, ge: (xt[g * MOE_SUB] * MOE_TM, 0))

    w_in_spec = pl.BlockSpec((None, D, MOE_CH),
                             lambda g, c, xt, ot, nv, ge: (ge[g], 0, chunk_of(g, c, nv)))
    b_in_spec = pl.BlockSpec((None, 1, MOE_CH),
                             lambda g, c, xt, ot, nv, ge: (ge[g], 0, chunk_of(g, c, nv)))
    w_out_spec = pl.BlockSpec((None, MOE_CH, D),
                              lambda g, c, xt, ot, nv, ge: (ge[g], chunk_of(g, c, nv), 0))
    b_out_spec = pl.BlockSpec((None, 1, D), lambda g, c, xt, ot, nv, ge: (ge[g], 0, 0))
    return pl.pallas_call(
        _moe_kernel,
        grid_spec=pltpu.PrefetchScalarGridSpec(
            num_scalar_prefetch=4, grid=(MOE_GROUPS, MOE_NCH),
            in_specs=[x_spec, w_in_spec, w_in_spec, w_out_spec, b_in_spec, b_in_spec, b_out_spec],
            out_specs=pl.BlockSpec(memory_space=pl.ANY),
            scratch_shapes=[pltpu.VMEM((D, MOE_CH), BF16), pltpu.VMEM((D, MOE_CH), BF16),
                            pltpu.VMEM((MOE_CH, D), BF16),
                            pltpu.VMEM((2, MOE_SUB, MOE_TM, D), F32),
                            pltpu.SemaphoreType.DMA((2,))]),
        out_shape=jax.ShapeDtypeStruct((MOE_ROWS, D), F32),
        compiler_params=_cparams(("arbitrary", "arbitrary"), 62),
        name="moe_experts",
    )(x_tile, o_tile, n_valid, g_exp, xs, wg, wu, wd,
      bg.reshape(N_EXP, 1, D_FF), bu.reshape(N_EXP, 1, D_FF), bd.reshape(N_EXP, 1, D))


COMB_TM = 256


COMB_TILES = N_TOK // COMB_TM
COMB_BLK = 8


def _combine_kernel(slot_ref, ys_hbm, w_ref, x1_ref, g2_ref, npost_ref, oc_ref, ol_ref, buf, sem):
    i = pl.program_id(0)

    def start_tile(b):
        def issue(blk, carry):
            for kk in range(TOP_K):
                for p in range(COMB_BLK):
                    r = blk * COMB_BLK + p
                    _row_copy(ys_hbm, slot_ref[0, kk * COMB_TM + r], buf.at[b, kk], r,
                              sem.at[b]).start()
            return carry

        lax.fori_loop(0, COMB_TM // COMB_BLK, issue, 0)

    def wait_tile(b):
        def wait(blk, carry):
            for kk in range(TOP_K):
                for p in range(COMB_BLK):
                    _row_copy(ys_hbm, 0, buf.at[b, kk], blk * COMB_BLK + p, sem.at[b]).wait()
            return carry

        lax.fori_loop(0, COMB_TM // COMB_BLK, wait, 0)

    for par in range(2):
        @pl.when(jnp.logical_and(i < COMB_TILES, (i & 1) == par))
        def _(par=par):
            start_tile(par)

    for par in range(2):
        @pl.when(jnp.logical_and(i >= 1, ((i - 1) & 1) == par))
        def _(par=par):
            wait_tile(par)

    @pl.when(i >= 1)
    def _():
        b = (i - 1) & 1
        w = w_ref[...]
        f = buf[b, 0] * w[:, 0:1]
        for kk in range(1, TOP_K):
            f = f + buf[b, kk] * w[:, kk:kk + 1]
        out = x1_ref[...] + g2_ref[...] * _rms(f, npost_ref[...])
        base = (i - 1) * COMB_TM

        @pl.when(base < N_CTX)
        def _():
            oc_ref[...] = out

        @pl.when(base >= N_CTX)
        def _():
            ol_ref[...] = out


def _combine(ys, slot, top_w_t, x1, mod3, npost):
    tm = COMB_TM

    def prev(i):
        return jnp.maximum(i - 1, 0)

    oc_spec, ol_spec = _ctx_lat_specs(tm, D)
    return pl.pallas_call(
        _combine_kernel,
        grid=(COMB_TILES + 1,),
        in_specs=[pl.BlockSpec((None, 1, TOP_K * tm),
                               lambda i: (jnp.minimum(i, COMB_TILES - 1), 0, 0),
                               memory_space=pltpu.SMEM),
                  pl.BlockSpec(memory_space=pl.ANY),
                  pl.BlockSpec((tm, TOP_K), lambda i: (prev(i), 0)),
                  pl.BlockSpec((tm, D), lambda i: (prev(i), 0)),
                  pl.BlockSpec((None, 1, D), lambda i: (_mod_row(prev(i) * tm), 0, MOD_G2)),
                  pl.BlockSpec((1, D), lambda i: (0, 0))],
        out_specs=(pl.BlockSpec((tm, D), lambda i: oc_spec.index_map(prev(i))),
                   pl.BlockSpec((tm, D), lambda i: ol_spec.index_map(prev(i)))),
        scratch_shapes=[pltpu.VMEM((2, TOP_K, tm, D), F32), pltpu.SemaphoreType.DMA((2,))],
        out_shape=(jax.ShapeDtypeStruct((N_CTX, D), F32), jax.ShapeDtypeStruct((N_LAT, D), F32)),
        compiler_params=_cparams(("arbitrary",), 48),
        name="moe_combine",
    )(slot.reshape(TOP_K, COMB_TILES, tm).transpose(1, 0, 2).reshape(COMB_TILES, 1, TOP_K * tm),
      ys, top_w_t, x1, mod3, npost.reshape(1, D))


def _rope_tables():
    pos = jnp.arange(DEC_SEQ)
    row = (pos // GRID_W).astype(F32)
    col = (pos % GRID_W).astype(F32)
    n_freq = DK // 4
    freqs = THETA ** (-jnp.arange(n_freq, dtype=F32) / n_freq)
    ar = row[:, None] * freqs
    ac = col[:, None] * freqs
    cos_t = jnp.concatenate([jnp.cos(ar), jnp.cos(ar), jnp.cos(ac), jnp.cos(ac)], axis=1)
    sin_t = jnp.concatenate([-jnp.sin(ar), jnp.sin(ar), -jnp.sin(ac), jnp.sin(ac)], axis=1)
    return cos_t.astype(F32), sin_t.astype(F32)


def kernel(x_prompt, x_sample, state_ret, c, c_ctx, w_ada, b_ada, norm_mix_pre, norm_mix_post,
           norm_ffn_pre, norm_ffn_post, w_in, conv_w, conv_b, ret_decay_logit, ret_gn_w,
           w_br_conv, w_br_ret, w_out, router_w, router_b, moe_w_gate, moe_b_gate,
           moe_w_up, moe_b_up, moe_w_down, moe_b_down):
    layer = 0
    x_ctx = x_prompt.reshape(N_CTX, D)
    x_lat = x_sample.reshape(N_LAT, D)
    cvec = jnp.concatenate(
        [c_ctx[None, :], c, jnp.zeros((MOD_ROWS - 1 - DEC_BATCH, D), F32)], axis=0)

    mod = _modulation(cvec, w_ada[layer], b_ada[layer])
    mod3 = mod.reshape(MOD_ROWS, 1, N_MOD * D)

    h1 = _norm_mod(x_ctx, x_lat, norm_mix_pre[layer], mod3)
    proj = _in_proj(h1, w_in[layer])

    a = _conv_branch(proj, conv_w[layer], conv_b[layer], w_br_conv[layer].astype(BF16))

    cos_t, sin_t = _rope_tables()
    dl = ret_decay_logit[layer].astype(F32)
    r_ctx, new_state = _retention(proj, dl, ret_gn_w[layer], None, None, None, latent=False)
    r_lat = _retention(proj, dl, ret_gn_w[layer], cos_t, sin_t, state_ret, latent=True)

    x1, h2, top_i, top_w = _mix_route(
        r_ctx, r_lat, w_br_ret[layer].astype(BF16), a, proj, w_out[layer].astype(BF16),
        x_ctx, x_lat, mod3, norm_mix_post[layer], norm_ffn_pre[layer],
        router_w[layer].T.astype(BF16), router_b[layer])

    slot, tok_of_slot, tile_rows, n_tiles, x_tile, o_tile, n_valid, g_exp = _routing_tables(top_i)
    xs = _dispatch(h2, tok_of_slot, n_tiles, tile_rows)
    ys = _moe(xs, moe_w_gate[layer], moe_w_up[layer], moe_w_down[layer],
              moe_b_gate[layer], moe_b_up[layer], moe_b_down[layer],
              x_tile, o_tile, n_valid, g_exp)
    y_ctx, y_lat = _combine(ys, slot, top_w.T, x1, mod3, norm_ffn_post[layer])

    return (y_ctx.reshape(BATCH, SEQ, D), y_lat.reshape(DEC_BATCH, DEC_SEQ, D), new_state)
```
